```python
import jax, jax.numpy as jnp
from jax import lax
import numpy as np

D_MODEL = 2048
BATCH = 4
SEQ = 4096
DEPTH = 1

N_ATTN_HEADS = 8
ATTN_HEAD_DIM = 128
ATTN_WIDTH = N_ATTN_HEADS * ATTN_HEAD_DIM
Q_BLOCK = 128
LRU_WIDTH = D_MODEL // 2
LRU_BLOCKS = 8
LRU_BLOCK = LRU_WIDTH // LRU_BLOCKS
CONV_WIDTH = 4
LRU_C = 8.0
D_FF = ((8 * D_MODEL // 3 + 255) // 256) * 256
N_MOD = 9
EPS = 1e-6

IN_WIDTHS = (ATTN_WIDTH, ATTN_WIDTH, ATTN_WIDTH, LRU_WIDTH, LRU_WIDTH, D_MODEL, D_MODEL)
IN_TOTAL = sum(IN_WIDTHS)
IN_SPLITS = tuple(int(s) for s in np.cumsum(IN_WIDTHS)[:-1])

kernel_name = "hybrid_stickbreak_rglru_macaron_adaln"


def rms_norm(x, g):
    xf = x.astype(jnp.float32)
    y = xf * lax.rsqrt(jnp.mean(xf * xf, axis=-1, keepdims=True) + EPS)
    return (y * g.astype(jnp.float32)).astype(x.dtype)


def modulate(x, shift, scale):
    return x * (1.0 + scale[:, None, :]) + shift[:, None, :]


def swiglu(x, w_in, w_out):
    gate, up = jnp.split(x @ w_in, 2, axis=-1)
    return (jax.nn.silu(gate) * up) @ w_out


def causal_depthwise_conv(x, w, b):
    y = lax.conv_general_dilated(
        x, w[:, None, :].astype(x.dtype), window_strides=(1,),
        padding=[(CONV_WIDTH - 1, 0)],
        dimension_numbers=("NWC", "WIO", "NWC"),
        feature_group_count=x.shape[-1])
    return y + b


def stick_breaking_attention(q, k, v):
    S = q.shape[2]
    scale = ATTN_HEAD_DIM ** -0.5
    outs = []
    for blk in range(S // Q_BLOCK):
        q0 = blk * Q_BLOCK
        n_keys = q0 + Q_BLOCK
        qb = q[:, :, q0:n_keys]
        kb = k[:, :, :n_keys]
        vb = v[:, :, :n_keys]
        z = jnp.einsum("bhqd,bhkd->bhqk", qb, kb).astype(jnp.float32) * scale
        t_idx = q0 + jnp.arange(Q_BLOCK)[:, None]
        s_idx = jnp.arange(n_keys)[None, :]
        before = s_idx < t_idx
        log_keep = jnp.where(before, jax.nn.log_sigmoid(-z), 0.0)
        rev = lax.cumsum(log_keep, axis=3, reverse=True)
        between = jnp.concatenate([rev[..., 1:], jnp.zeros_like(rev[..., :1])], axis=-1)
        w = jnp.where(before, jnp.exp(jax.nn.log_sigmoid(z) + between), 0.0)
        outs.append(jnp.einsum("bhqk,bhkd->bhqd", w.astype(vb.dtype), vb))
    return jnp.concatenate(outs, axis=2)


def rg_lru(x, w_r, b_r, w_i, b_i, lam):
    B, S, C = x.shape
    xb = x.reshape(B, S, LRU_BLOCKS, LRU_BLOCK)
    r = jax.nn.sigmoid(jnp.einsum("bsnc,ncd->bsnd", xb, w_r).reshape(B, S, C) + b_r)
    i = jax.nn.sigmoid(jnp.einsum("bsnc,ncd->bsnd", xb, w_i).reshape(B, S, C) + b_i)
    log_a = -LRU_C * r.astype(jnp.float32) * jax.nn.softplus(-lam.astype(jnp.float32))
    a = jnp.exp(log_a)
    u = jnp.sqrt(-jnp.expm1(2.0 * log_a)) * (i * x).astype(jnp.float32)

    def combine(left, right):
        a_l, b_l = left
        a_r, b_r2 = right
        return a_l * a_r, a_r * b_l + b_r2

    _, h = lax.associative_scan(combine, (a, u), axis=1)
    return h.astype(x.dtype)


def hybrid_mixer(h, w_in, conv_w, conv_b, w_r, b_r, w_i, b_i, lam, w_br_attn, w_br_lru, w_out):
    B, S, _ = h.shape
    q, k, v, xr, gr, g_attn, g_lru = jnp.split(h @ w_in, IN_SPLITS, axis=-1)

    def heads(t):
        return t.reshape(B, S, N_ATTN_HEADS, ATTN_HEAD_DIM).transpose(0, 2, 1, 3)

    y_attn = stick_breaking_attention(heads(q), heads(k), heads(v))
    y_attn = y_attn.transpose(0, 2, 1, 3).reshape(B, S, ATTN_WIDTH)

    xr = causal_depthwise_conv(xr, conv_w, conv_b)
    y_lru = rg_lru(xr, w_r, b_r, w_i, b_i, lam) * jax.nn.gelu(gr)

    merged = (jax.nn.sigmoid(g_attn) * (y_attn @ w_br_attn)
              + jax.nn.sigmoid(g_lru) * (y_lru @ w_br_lru))
    return merged @ w_out


def setup_inputs(seed: int = 0) -> dict:
    key = jax.random.key(seed)
    ks = jax.random.split(key, 24)
    f32 = jnp.float32

    def nrm(k, shape, std):
        return jax.random.normal(k, shape, f32) * std

    u = jax.random.uniform(ks[13], (DEPTH, LRU_WIDTH), f32, 0.9, 0.999)
    a0 = u ** (1.0 / LRU_C)
    lam = jnp.log(a0) - jnp.log1p(-a0)

    return {
        "x": nrm(ks[0], (BATCH, SEQ, D_MODEL), 1.0),
        "c": nrm(ks[1], (BATCH, D_MODEL), 1.0),
        "w_ada": nrm(ks[2], (DEPTH, D_MODEL, N_MOD * D_MODEL), 0.5 * D_MODEL ** -0.5),
        "b_ada": nrm(ks[3], (DEPTH, N_MOD * D_MODEL), 0.01),
        "norm_ffn1": 1.0 + nrm(ks[4], (DEPTH, D_MODEL), 0.01),
        "w_ffn1_in": nrm(ks[5], (DEPTH, D_MODEL, 2 * D_FF), D_MODEL ** -0.5),
        "w_ffn1_out": nrm(ks[6], (DEPTH, D_FF, D_MODEL), D_FF ** -0.5),
        "norm_mix": 1.0 + nrm(ks[7], (DEPTH, D_MODEL), 0.01),
        "w_in": nrm(ks[8], (DEPTH, D_MODEL, IN_TOTAL), D_MODEL ** -0.5),
        "conv_w": nrm(ks[9], (DEPTH, CONV_WIDTH, LRU_WIDTH), CONV_WIDTH ** -0.5),
        "conv_b": nrm(ks[10], (DEPTH, LRU_WIDTH), 0.01),
        "w_rg_gate": nrm(ks[11], (DEPTH, LRU_BLOCKS, LRU_BLOCK, LRU_BLOCK), LRU_BLOCK ** -0.5),
        "b_rg_gate": nrm(ks[12], (DEPTH, LRU_WIDTH), 0.01),
        "w_in_gate": nrm(ks[14], (DEPTH, LRU_BLOCKS, LRU_BLOCK, LRU_BLOCK), LRU_BLOCK ** -0.5),
        "b_in_gate": nrm(ks[15], (DEPTH, LRU_WIDTH), 0.01),
        "lru_lambda": lam,
        "w_branch_attn": nrm(ks[16], (DEPTH, ATTN_WIDTH, D_MODEL), ATTN_WIDTH ** -0.5),
        "w_branch_lru": nrm(ks[17], (DEPTH, LRU_WIDTH, D_MODEL), LRU_WIDTH ** -0.5),
        "w_out": nrm(ks[18], (DEPTH, D_MODEL, D_MODEL), D_MODEL ** -0.5),
        "norm_ffn2": 1.0 + nrm(ks[19], (DEPTH, D_MODEL), 0.01),
        "w_ffn2_in": nrm(ks[20], (DEPTH, D_MODEL, 2 * D_FF), D_MODEL ** -0.5),
        "w_ffn2_out": nrm(ks[21], (DEPTH, D_FF, D_MODEL), D_FF ** -0.5),
        "norm_final": 1.0 + nrm(ks[22], (D_MODEL,), 0.01),
    }


def reference(x, c, w_ada, b_ada, norm_ffn1, w_ffn1_in, w_ffn1_out, norm_mix, w_in,
              conv_w, conv_b, w_rg_gate, b_rg_gate, w_in_gate, b_in_gate, lru_lambda,
              w_branch_attn, w_branch_lru, w_out, norm_ffn2, w_ffn2_in, w_ffn2_out,
              norm_final):
    h = x
    c_act = jax.nn.silu(c)
    for l in range(DEPTH):
        mod = c_act @ w_ada[l] + b_ada[l]
        sh1, sc1, g1, sh2, sc2, g2, sh3, sc3, g3 = jnp.split(mod, N_MOD, axis=-1)

        y = modulate(rms_norm(h, norm_ffn1[l]), sh1, sc1)
        h = h + 0.5 * g1[:, None, :] * swiglu(y, w_ffn1_in[l], w_ffn1_out[l])

        y = modulate(rms_norm(h, norm_mix[l]), sh2, sc2)
        h = h + g2[:, None, :] * hybrid_mixer(
            y, w_in[l], conv_w[l], conv_b[l], w_rg_gate[l], b_rg_gate[l],
            w_in_gate[l], b_in_gate[l], lru_lambda[l],
            w_branch_attn[l], w_branch_lru[l], w_out[l])

        y = modulate(rms_norm(h, norm_ffn2[l]), sh3, sc3)
        h = h + 0.5 * g3[:, None, :] * swiglu(y, w_ffn2_in[l], w_ffn2_out[l])
    return rms_norm(h, norm_final)
```

```python
import functools

import jax
import jax.numpy as jnp
from jax import lax
from jax.experimental import pallas as pl
from jax.experimental.pallas import tpu as pltpu

F32 = jnp.float32
BF16 = jnp.bfloat16

EPS = 1e-6
N_MOD = 9
N_ATTN_HEADS = 8
ATTN_HEAD_DIM = 128
LRU_BLOCKS = 8
CONV_WIDTH = 4
LRU_C = 8.0

V7X_LANES = 128
V7X_SUBLANES = 8
V7X_VMEM_LIMIT_BYTES = 56 * 1024 * 1024

MOD_TN = 1024
FFN_TM = 512
FFN_TF = 512
INPROJ_TM = 512
INPROJ_TN = 1024
ATTN_T = 256
LRU_TS = 512
MERGE_TM = 256


def _params(*sem):
    return pltpu.CompilerParams(dimension_semantics=sem,
                                vmem_limit_bytes=V7X_VMEM_LIMIT_BYTES)


def _softplus(x):
    return jnp.maximum(x, 0.0) + jnp.log(1.0 + jnp.exp(-jnp.abs(x)))


def _norm_modulate(x, gain, shift, scale):
    y = x * lax.rsqrt(jnp.mean(x * x, axis=-1, keepdims=True) + EPS)
    y = y * gain
    return y * (1.0 + scale) + shift


def _mod_kernel(c_ref, w_ref, b_ref, o_ref):
    c = c_ref[...]
    c_act = (c * jax.nn.sigmoid(c)).astype(BF16)
    o_ref[...] = jnp.dot(c_act, w_ref[...].astype(BF16),
                         preferred_element_type=F32) + b_ref[...]


def _modulation(c, w_ada, b_ada):
    batch, d = c.shape
    n = w_ada.shape[1]
    return pl.pallas_call(
        _mod_kernel,
        out_shape=jax.ShapeDtypeStruct((batch, n), F32),
        grid=(n // MOD_TN,),
        in_specs=[pl.BlockSpec((batch, d), lambda j: (0, 0)),
                  pl.BlockSpec((d, MOD_TN), lambda j: (0, j)),
                  pl.BlockSpec((1, MOD_TN), lambda j: (0, j))],
        out_specs=pl.BlockSpec((batch, MOD_TN), lambda j: (0, j)),
        compiler_params=_params("arbitrary"),
        name="adaln_mod",
    )(c, w_ada, b_ada.reshape(1, n))


def _ffn_kernel(x_ref, mod_ref, gain_ref, wg_ref, wu_ref, wo_ref, fgain_ref,
                o_ref, y_scr, *, final_norm):
    j = pl.program_id(1)

    @pl.when(j == 0)
    def _():
        y = _norm_modulate(x_ref[...], gain_ref[...], mod_ref[0:1, :], mod_ref[1:2, :])
        y_scr[...] = y.astype(BF16)

    y = y_scr[...]
    g = jnp.dot(y, wg_ref[...], preferred_element_type=F32)
    u = jnp.dot(y, wu_ref[...], preferred_element_type=F32)
    a = (g * jax.nn.sigmoid(g) * u).astype(BF16)
    part = jnp.dot(a, wo_ref[...], preferred_element_type=F32)

    @pl.when(j == 0)
    def _():
        o_ref[...] = part

    @pl.when(j > 0)
    def _():
        o_ref[...] += part

    @pl.when(j == pl.num_programs(1) - 1)
    def _():
        h = x_ref[...] + (0.5 * mod_ref[2:3, :]) * o_ref[...]
        if final_norm:
            h = h * lax.rsqrt(jnp.mean(h * h, axis=-1, keepdims=True) + EPS)
            h = h * fgain_ref[...]
        o_ref[...] = h


def _ffn(x2d, mod4, which, gain, w_in, w_out, fgain, seq, final_norm):
    m, d = x2d.shape
    f = w_out.shape[0]
    nf = f // FFN_TF
    blocks_per_seq = seq // FFN_TM
    return pl.pallas_call(
        functools.partial(_ffn_kernel, final_norm=final_norm),
        out_shape=jax.ShapeDtypeStruct((m, d), F32),
        grid=(m // FFN_TM, nf),
        in_specs=[
            pl.BlockSpec((FFN_TM, d), lambda i, j: (i, 0)),
            pl.BlockSpec((None, None, 3, d),
                         lambda i, j: (i // blocks_per_seq, which, 0, 0)),
            pl.BlockSpec((1, d), lambda i, j: (0, 0)),
            pl.BlockSpec((d, FFN_TF), lambda i, j: (0, j)),
            pl.BlockSpec((d, FFN_TF), lambda i, j: (0, j + nf)),
            pl.BlockSpec((FFN_TF, d), lambda i, j: (j, 0)),
            pl.BlockSpec((1, d), lambda i, j: (0, 0)),
        ],
        out_specs=pl.BlockSpec((FFN_TM, d), lambda i, j: (i, 0)),
        scratch_shapes=[pltpu.VMEM((FFN_TM, d), BF16)],
        compiler_params=_params("arbitrary", "arbitrary"),
        name="ffn_final" if final_norm else "ffn",
    )(x2d, mod4, gain.reshape(1, d), w_in, w_in, w_out, fgain.reshape(1, d))


def _inproj_kernel(h_ref, mod_ref, gain_ref, w_ref, qkv_ref, rest_ref, y_scr,
                   *, n_qkv_blocks):
    j = pl.program_id(1)

    @pl.when(j == 0)
    def _():
        y = _norm_modulate(h_ref[...], gain_ref[...], mod_ref[0:1, :], mod_ref[1:2, :])
        y_scr[...] = y.astype(BF16)

    r = jnp.dot(y_scr[...], w_ref[...], preferred_element_type=F32)

    @pl.when(j < n_qkv_blocks)
    def _():
        qkv_ref[...] = r.astype(BF16)

    @pl.when(j >= n_qkv_blocks)
    def _():
        rest_ref[...] = r


def _inproj(h2d, mod4, gain, w_in, seq, qkv_width):
    m, d = h2d.shape
    n = w_in.shape[1]
    nq = qkv_width // INPROJ_TN
    blocks_per_seq = seq // INPROJ_TM
    return pl.pallas_call(
        functools.partial(_inproj_kernel, n_qkv_blocks=nq),
        out_shape=(jax.ShapeDtypeStruct((m, qkv_width), BF16),
                   jax.ShapeDtypeStruct((m, n - qkv_width), F32)),
        grid=(m // INPROJ_TM, n // INPROJ_TN),
        in_specs=[
            pl.BlockSpec((INPROJ_TM, d), lambda i, j: (i, 0)),
            pl.BlockSpec((None, None, 3, d),
                         lambda i, j: (i // blocks_per_seq, 1, 0, 0)),
            pl.BlockSpec((1, d), lambda i, j: (0, 0)),
            pl.BlockSpec((d, INPROJ_TN), lambda i, j: (0, j)),
        ],
        out_specs=(
            pl.BlockSpec((INPROJ_TM, INPROJ_TN),
                         lambda i, j: (i, jnp.minimum(j, nq - 1))),
            pl.BlockSpec((INPROJ_TM, INPROJ_TN),
                         lambda i, j: (i, jnp.maximum(j - nq, 0))),
        ),
        scratch_shapes=[pltpu.VMEM((INPROJ_TM, d), BF16)],
        compiler_params=_params("arbitrary", "arbitrary"),
        name="mixer_inproj",
    )(h2d, mod4, gain.reshape(1, d), w_in)


def _attn_kernel(q_ref, k_ref, v_ref, tri_ref, o_ref):
    i = pl.program_id(2)
    t = ATTN_T
    q = q_ref[...]
    tri = tri_ref[...]
    scale = ATTN_HEAD_DIM ** -0.5

    def block(kb, vb, carry, before):
        z = lax.dot_general(q, kb, (((1,), (1,)), ((), ())),
                            preferred_element_type=F32) * scale
        log_keep = -_softplus(z)
        if before is not None:
            log_keep = jnp.where(before, log_keep, 0.0)
        hi = log_keep.astype(BF16)
        lo = (log_keep - hi.astype(F32)).astype(BF16)
        csum = (jnp.dot(hi, tri, preferred_element_type=F32)
                + jnp.dot(lo, tri, preferred_element_type=F32))
        w = jnp.exp(z + csum + carry)
        if before is not None:
            w = jnp.where(before, w, 0.0)
        pv = jnp.dot(w.astype(BF16), vb, preferred_element_type=F32)
        return pv, carry + csum[:, 0:1]

    row = lax.broadcasted_iota(jnp.int32, (t, t), 0)
    col = lax.broadcasted_iota(jnp.int32, (t, t), 1)
    d0 = pl.multiple_of(i * t, t)
    acc, carry = block(k_ref[pl.ds(d0, t), :], v_ref[pl.ds(d0, t), :],
                       jnp.zeros((t, 1), F32), col < row)

    def body(n, state):
        acc, carry = state
        k0 = pl.multiple_of((i - 1 - n) * t, t)
        pv, carry = block(k_ref[pl.ds(k0, t), :], v_ref[pl.ds(k0, t), :], carry, None)
        return acc + pv, carry

    acc, _ = lax.fori_loop(0, i, body, (acc, carry))
    o_ref[...] = acc.astype(o_ref.dtype)


def _attention(qkv, batch, seq):
    m = qkv.shape[0]
    t = ATTN_T
    nq = seq // t
    width = N_ATTN_HEADS * ATTN_HEAD_DIM
    idx = jnp.arange(t)
    upper = (idx[:, None] >= idx[None, :]).astype(BF16)
    return pl.pallas_call(
        _attn_kernel,
        out_shape=jax.ShapeDtypeStruct((m, width), BF16),
        grid=(batch, N_ATTN_HEADS, nq),
        in_specs=[
            pl.BlockSpec((t, ATTN_HEAD_DIM), lambda b, h, i: (b * nq + i, h)),
            pl.BlockSpec((seq, ATTN_HEAD_DIM), lambda b, h, i: (b, N_ATTN_HEADS + h)),
            pl.BlockSpec((seq, ATTN_HEAD_DIM), lambda b, h, i: (b, 2 * N_ATTN_HEADS + h)),
            pl.BlockSpec((t, t), lambda b, h, i: (0, 0)),
        ],
        out_specs=pl.BlockSpec((t, ATTN_HEAD_DIM), lambda b, h, i: (b * nq + i, h)),
        compiler_params=_params("arbitrary", "arbitrary", "arbitrary"),
        name="stickbreak_attn",
    )(qkv, qkv, qkv, upper)


def _lru_kernel(xr_ref, gr_ref, cw_ref, cb_ref, wg_ref, br_ref, bi_ref, lam_ref,
                o_ref, xbuf, hcar, a_scr, u_scr):
    ts = LRU_TS
    c = xr_ref.shape[1]
    blk = c // LRU_BLOCKS
    step = pl.program_id(1)
    halo = V7X_SUBLANES

    @pl.when(step == 0)
    def _():
        xbuf[0:halo, :] = jnp.zeros((halo, c), F32)
        hcar[...] = jnp.zeros_like(hcar)

    @pl.when(step > 0)
    def _():
        xbuf[0:halo, :] = xbuf[ts:ts + halo, :]

    xbuf[halo:ts + halo, :] = xr_ref[...]

    xc = cb_ref[...] + cw_ref[0:1, :] * xbuf[pl.ds(halo - (CONV_WIDTH - 1), ts), :]
    for k in range(1, CONV_WIDTH):
        xc = xc + cw_ref[k:k + 1, :] * xbuf[pl.ds(halo - (CONV_WIDTH - 1) + k, ts), :]

    sp = _softplus(-lam_ref[...])
    for n in range(LRU_BLOCKS):
        sl = slice(n * blk, (n + 1) * blk)
        xn = xc[:, sl]
        gates = jnp.dot(xn.astype(BF16), wg_ref[n], preferred_element_type=F32)
        r = jax.nn.sigmoid(gates[:, :blk] + br_ref[:, sl])
        ig = jax.nn.sigmoid(gates[:, blk:] + bi_ref[:, sl])
        log_a = (-LRU_C * r) * sp[:, sl]
        a = jnp.exp(log_a)
        a_scr[:, sl] = a
        u_scr[:, sl] = jnp.sqrt(-jnp.tanh(log_a) * (1.0 + a * a)) * (ig * xn)

    row = lax.broadcasted_iota(jnp.int32, (V7X_SUBLANES, c), 0)

    def body(g, h):
        r0 = pl.multiple_of(g * V7X_SUBLANES, V7X_SUBLANES)
        a = a_scr[pl.ds(r0, V7X_SUBLANES), :]
        u = u_scr[pl.ds(r0, V7X_SUBLANES), :]
        for sh in (1, 2, 4):
            a_prev = jnp.where(row >= sh, pltpu.roll(a, sh, 0), 1.0)
            u_prev = jnp.where(row >= sh, pltpu.roll(u, sh, 0), 0.0)
            u = a * u_prev + u
            a = a * a_prev
        hh = a * h + u
        u_scr[pl.ds(r0, V7X_SUBLANES), :] = hh
        return hh[V7X_SUBLANES - 1:V7X_SUBLANES, :]

    hcar[...] = lax.fori_loop(0, ts // V7X_SUBLANES, body, hcar[...])
    o_ref[...] = (u_scr[...] * jax.nn.gelu(gr_ref[...])).astype(o_ref.dtype)


def _lru(rest, conv_w, conv_b, w_gates, b_r, b_i, lam, batch, seq):
    m = rest.shape[0]
    c = conv_w.shape[1]
    nt = seq // LRU_TS
    return pl.pallas_call(
        _lru_kernel,
        out_shape=jax.ShapeDtypeStruct((m, c), BF16),
        grid=(batch, nt),
        in_specs=[
            pl.BlockSpec((LRU_TS, c), lambda b, s: (b * nt + s, 0)),
            pl.BlockSpec((LRU_TS, c), lambda b, s: (b * nt + s, 1)),
            pl.BlockSpec((CONV_WIDTH, c), lambda b, s: (0, 0)),
            pl.BlockSpec((1, c), lambda b, s: (0, 0)),
            pl.BlockSpec(w_gates.shape, lambda b, s: (0, 0, 0)),
            pl.BlockSpec((1, c), lambda b, s: (0, 0)),
            pl.BlockSpec((1, c), lambda b, s: (0, 0)),
            pl.BlockSpec((1, c), lambda b, s: (0, 0)),
        ],
        out_specs=pl.BlockSpec((LRU_TS, c), lambda b, s: (b * nt + s, 0)),
        scratch_shapes=[pltpu.VMEM((LRU_TS + V7X_SUBLANES, c), F32),
                        pltpu.VMEM((1, c), F32),
                        pltpu.VMEM((LRU_TS, c), F32),
                        pltpu.VMEM((LRU_TS, c), F32)],
        compiler_params=_params("arbitrary", "arbitrary"),
        name="rg_lru",
    )(rest, rest, conv_w, conv_b.reshape(1, c), w_gates, b_r.reshape(1, c),
      b_i.reshape(1, c), lam.reshape(1, c))


def _merge_kernel(h_ref, ya_ref, yl_ref, ga_ref, gl_ref, mod_ref, wa_ref, wl_ref,
                  wo_ref, o_ref):
    pa = jnp.dot(ya_ref[...], wa_ref[...], preferred_element_type=F32)
    plru = jnp.dot(yl_ref[...], wl_ref[...], preferred_element_type=F32)
    merged = jax.nn.sigmoid(ga_ref[...]) * pa + jax.nn.sigmoid(gl_ref[...]) * plru
    out = jnp.dot(merged.astype(BF16), wo_ref[...], preferred_element_type=F32)
    o_ref[...] = h_ref[...] + mod_ref[2:3, :] * out


def _merge(h2d, y_attn, y_lru, rest, mod4, w_ba, w_bl, w_o, seq):
    m, d = h2d.shape
    wa = y_attn.shape[1]
    wl = y_lru.shape[1]
    gate_col0 = (wl * 2) // d
    blocks_per_seq = seq // MERGE_TM
    resident = pl.Buffered(1)
    return pl.pallas_call(
        _merge_kernel,
        out_shape=jax.ShapeDtypeStruct((m, d), F32),
        grid=(m // MERGE_TM,),
        in_specs=[
            pl.BlockSpec((MERGE_TM, d), lambda i: (i, 0)),
            pl.BlockSpec((MERGE_TM, wa), lambda i: (i, 0)),
            pl.BlockSpec((MERGE_TM, wl), lambda i: (i, 0)),
            pl.BlockSpec((MERGE_TM, d), lambda i: (i, gate_col0)),
            pl.BlockSpec((MERGE_TM, d), lambda i: (i, gate_col0 + 1)),
            pl.BlockSpec((None, None, 3, d), lambda i: (i // blocks_per_seq, 1, 0, 0)),
            pl.BlockSpec((wa, d), lambda i: (0, 0), pipeline_mode=resident),
            pl.BlockSpec((wl, d), lambda i: (0, 0), pipeline_mode=resident),
            pl.BlockSpec((d, d), lambda i: (0, 0), pipeline_mode=resident),
        ],
        out_specs=pl.BlockSpec((MERGE_TM, d), lambda i: (i, 0)),
        compiler_params=_params("arbitrary"),
        name="merge_outproj",
    )(h2d, y_attn, y_lru, rest, rest, mod4, w_ba, w_bl, w_o)


def kernel(x, c, w_ada, b_ada, norm_ffn1, w_ffn1_in, w_ffn1_out, norm_mix, w_in,
           conv_w, conv_b, w_rg_gate, b_rg_gate, w_in_gate, b_in_gate, lru_lambda,
           w_branch_attn, w_branch_lru, w_out, norm_ffn2, w_ffn2_in, w_ffn2_out,
           norm_final):
    batch, seq, d = x.shape
    depth = w_ada.shape[0]
    assert depth >= 1, "the final RMSNorm is fused into the last layer"
    attn_width = N_ATTN_HEADS * ATTN_HEAD_DIM
    h = x.reshape(batch * seq, d)
    ones = jnp.ones((d,), F32)
    for l in range(depth):
        last = l == depth - 1
        mod = _modulation(c, w_ada[l], b_ada[l])
        mod4 = mod.reshape(batch, N_MOD // 3, 3, d)

        h = _ffn(h, mod4, 0, norm_ffn1[l], w_ffn1_in[l].astype(BF16),
                 w_ffn1_out[l].astype(BF16), ones, seq, False)

        qkv, rest = _inproj(h, mod4, norm_mix[l], w_in[l].astype(BF16), seq,
                            3 * attn_width)
        y_attn = _attention(qkv, batch, seq)
        w_gates = jnp.concatenate([w_rg_gate[l], w_in_gate[l]], axis=-1).astype(BF16)
        y_lru = _lru(rest, conv_w[l], conv_b[l], w_gates, b_rg_gate[l], b_in_gate[l],
                     lru_lambda[l], batch, seq)
        h = _merge(h, y_attn, y_lru, rest, mod4, w_branch_attn[l].astype(BF16),
                   w_branch_lru[l].astype(BF16), w_out[l].astype(BF16), seq)

        h = _ffn(h, mod4, 2, norm_ffn2[l], w_ffn2_in[l].astype(BF16),
                 w_ffn2_out[l].astype(BF16), norm_final if last else ones, seq, last)
    return h.reshape(batch, seq, d)
```

```python
import functools

import jax
import jax.numpy as jnp
from jax import lax
from jax.experimental import pallas as pl
from jax.experimental.pallas import tpu as pltpu

F32 = jnp.float32
BF16 = jnp.bfloat16

EPS = 1e-6
N_MOD = 9
N_ATTN_HEADS = 8
ATTN_HEAD_DIM = 128
LRU_BLOCKS = 8
CONV_WIDTH = 4
LRU_C = 8.0
LOG2E = 1.4426950408889634

V7X_LANES = 128
V7X_SUBLANES = 8
V7X_VMEM_LIMIT_BYTES = 56 * 1024 * 1024

MOD_TN = 1024
FFN_TM = 512
FFN_TF = 512
FFN_CHAINS = 2
INPROJ_TM = 512
INPROJ_TN = 1024
ATTN_T = 256
ATTN_SKEW = (2, 3, 4)
LRU_TS = 512


def _params(*sem):
    return pltpu.CompilerParams(dimension_semantics=sem,
                                vmem_limit_bytes=V7X_VMEM_LIMIT_BYTES)


def _softplus(x):
    return jnp.maximum(x, 0.0) + jnp.log(1.0 + jnp.exp(-jnp.abs(x)))


def _norm_modulate(x, gain, shift, scale):
    y = x * lax.rsqrt(jnp.mean(x * x, axis=-1, keepdims=True) + EPS)
    y = y * gain
    return y * (1.0 + scale) + shift


def _mod_kernel(c_ref, w_ref, b_ref, o_ref):
    c = c_ref[...]
    c_act = (c * jax.nn.sigmoid(c)).astype(BF16)
    o_ref[...] = jnp.dot(c_act, w_ref[...].astype(BF16),
                         preferred_element_type=F32) + b_ref[...]


def _modulation(c, w_ada, b_ada):
    batch, d = c.shape
    n = w_ada.shape[1]
    return pl.pallas_call(
        _mod_kernel,
        out_shape=jax.ShapeDtypeStruct((batch, n), F32),
        grid=(n // MOD_TN,),
        in_specs=[pl.BlockSpec((batch, d), lambda j: (0, 0)),
                  pl.BlockSpec((d, MOD_TN), lambda j: (0, j)),
                  pl.BlockSpec((1, MOD_TN), lambda j: (0, j))],
        out_specs=pl.BlockSpec((batch, MOD_TN), lambda j: (0, j)),
        compiler_params=_params("arbitrary"),
        name="adaln_mod",
    )(c, w_ada, b_ada.reshape(1, n))


def _ffn_kernel(x_ref, mod_ref, gain_ref, wg_ref, wu_ref, wo_ref, fgain_ref,
                o_ref, y_scr, *, final_norm):
    j = pl.program_id(1)

    @pl.when(j == 0)
    def _():
        y = _norm_modulate(x_ref[...], gain_ref[...], mod_ref[0:1, :], mod_ref[1:2, :])
        y_scr[...] = y.astype(BF16)
        o_ref[...] = jnp.zeros_like(o_ref)

    y = y_scr[...]
    width = wg_ref.shape[1] // FFN_CHAINS
    part = None
    for n in range(FFN_CHAINS):
        sl = slice(n * width, (n + 1) * width)
        g = jnp.dot(y, wg_ref[:, sl], preferred_element_type=F32)
        u = jnp.dot(y, wu_ref[:, sl], preferred_element_type=F32)
        a = (g * jax.nn.sigmoid(g) * u).astype(BF16)
        down = jnp.dot(a, wo_ref[sl, :], preferred_element_type=F32)
        part = down if part is None else part + down
    o_ref[...] += part

    @pl.when(j == pl.num_programs(1) - 1)
    def _():
        h = x_ref[...] + (0.5 * mod_ref[2:3, :]) * o_ref[...]
        if final_norm:
            h = h * lax.rsqrt(jnp.mean(h * h, axis=-1, keepdims=True) + EPS)
            h = h * fgain_ref[...]
        o_ref[...] = h


def _ffn(x2d, mod4, which, gain, w_in, w_out, fgain, seq, final_norm):
    m, d = x2d.shape
    f = w_out.shape[0]
    nf = f // FFN_TF
    blocks_per_seq = seq // FFN_TM
    return pl.pallas_call(
        functools.partial(_ffn_kernel, final_norm=final_norm),
        out_shape=jax.ShapeDtypeStruct((m, d), F32),
        grid=(m // FFN_TM, nf),
        in_specs=[
            pl.BlockSpec((FFN_TM, d), lambda i, j: (i, 0)),
            pl.BlockSpec((None, None, 3, d),
                         lambda i, j: (i // blocks_per_seq, which, 0, 0)),
            pl.BlockSpec((1, d), lambda i, j: (0, 0)),
            pl.BlockSpec((d, FFN_TF), lambda i, j: (0, j)),
            pl.BlockSpec((d, FFN_TF), lambda i, j: (0, j + nf)),
            pl.BlockSpec((FFN_TF, d), lambda i, j: (j, 0)),
            pl.BlockSpec((1, d), lambda i, j: (0, 0)),
        ],
        out_specs=pl.BlockSpec((FFN_TM, d), lambda i, j: (i, 0)),
        scratch_shapes=[pltpu.VMEM((FFN_TM, d), BF16)],
        compiler_params=_params("arbitrary", "arbitrary"),
        name="ffn_final" if final_norm else "ffn",
    )(x2d, mod4, gain.reshape(1, d), w_in, w_in, w_out, fgain.reshape(1, d))


def _inproj_kernel(h_ref, mod_ref, gain_ref, w_ref, o_ref, y_scr):
    @pl.when(pl.program_id(1) == 0)
    def _():
        y = _norm_modulate(h_ref[...], gain_ref[...], mod_ref[0:1, :], mod_ref[1:2, :])
        y_scr[...] = y.astype(BF16)

    o_ref[...] = jnp.dot(y_scr[...], w_ref[...],
                         preferred_element_type=F32).astype(o_ref.dtype)


def _inproj(h2d, mod4, gain, w_in, seq):
    m, d = h2d.shape
    n = w_in.shape[1]
    blocks_per_seq = seq // INPROJ_TM
    return pl.pallas_call(
        _inproj_kernel,
        out_shape=jax.ShapeDtypeStruct((m, n), BF16),
        grid=(m // INPROJ_TM, n // INPROJ_TN),
        in_specs=[
            pl.BlockSpec((INPROJ_TM, d), lambda i, j: (i, 0)),
            pl.BlockSpec((None, None, 3, d),
                         lambda i, j: (i // blocks_per_seq, 1, 0, 0)),
            pl.BlockSpec((1, d), lambda i, j: (0, 0)),
            pl.BlockSpec((d, INPROJ_TN), lambda i, j: (0, j)),
        ],
        out_specs=pl.BlockSpec((INPROJ_TM, INPROJ_TN), lambda i, j: (i, j)),
        scratch_shapes=[pltpu.VMEM((INPROJ_TM, d), BF16)],
        compiler_params=_params("arbitrary", "arbitrary"),
        name="mixer_inproj",
    )(h2d, mod4, gain.reshape(1, d), w_in)


def _attn_kernel(qa_ref, qb_ref, k_ref, v_ref, tri_ref, o_ref, q_scr, acc_scr, car_scr):
    p = pl.program_id(2)
    t = ATTN_T
    nq = k_ref.shape[0] // t
    yscale = ATTN_HEAD_DIM ** -0.5 * LOG2E
    row = lax.broadcasted_iota(jnp.int32, (t, t), 0)
    col = lax.broadcasted_iota(jnp.int32, (t, t), 1)
    before = col < row

    q_scr[0] = (qa_ref[...].astype(F32) * yscale).astype(BF16)
    q_scr[1] = (qb_ref[...].astype(F32) * yscale).astype(BF16)

    blocks = [(0, p, True), (1, nq - 1 - p, True)]
    for s in range(nq - 1):
        blocks.append((jnp.where(s >= p, 1, 0),
                       jnp.where(s >= p, nq - 2 - s, p - 1 - s), False))

    def scores(blk):
        side, kblk, _ = blk
        k0 = pl.multiple_of(kblk * t, t)
        return lax.dot_general(q_scr[side], k_ref[pl.ds(k0, t), :],
                               (((1,), (1,)), ((), ())), preferred_element_type=F32)

    def log_keep(blk, y):
        pos = jnp.maximum(y, 0.0) + jnp.log(1.0 + jnp.exp2(-jnp.abs(y))) * LOG2E
        if blk[2]:
            pos = jnp.where(before, pos, 0.0)
        return y, jnp.dot(pos.astype(BF16), tri_ref[...], preferred_element_type=F32)

    def weights(blk, y, csum):
        side, kblk, diag = blk
        total = jnp.broadcast_to(csum[:, 0:1], (t, V7X_LANES))
        if diag:
            car_scr[side] = total
            w = jnp.where(before, jnp.exp2(y - csum), 0.0)
        else:
            carry = car_scr[side]
            car_scr[side] = carry + total
            w = jnp.exp2(y - csum - jnp.concatenate([carry, carry], axis=1))
        k0 = pl.multiple_of(kblk * t, t)
        return jnp.dot(w.astype(BF16), v_ref[pl.ds(k0, t), :], preferred_element_type=F32)

    def accumulate(blk, pv):
        if blk[2]:
            acc_scr[blk[0]] = pv
        else:
            acc_scr[blk[0]] += pv

    nb = len(blocks)
    raw, ycs, pvs = {}, {}, {}
    d1, d2, d3 = ATTN_SKEW
    for step in range(nb + d3):
        if step < nb:
            raw[step] = scores(blocks[step])
        if 0 <= step - d1 < nb:
            ycs[step - d1] = log_keep(blocks[step - d1], raw.pop(step - d1))
        if 0 <= step - d2 < nb:
            pvs[step - d2] = weights(blocks[step - d2], *ycs.pop(step - d2))
        if 0 <= step - d3 < nb:
            accumulate(blocks[step - d3], pvs.pop(step - d3))

    o_ref[...] = acc_scr[...].astype(o_ref.dtype)


def _attention(proj, batch, seq):
    t = ATTN_T
    nq = seq // t
    width = N_ATTN_HEADS * ATTN_HEAD_DIM
    idx = jnp.arange(t)
    upper = (idx[:, None] >= idx[None, :]).astype(BF16)
    return pl.pallas_call(
        _attn_kernel,
        out_shape=jax.ShapeDtypeStruct((batch, nq // 2, 2, t, width), BF16),
        grid=(batch, N_ATTN_HEADS, nq // 2),
        in_specs=[
            pl.BlockSpec((t, ATTN_HEAD_DIM), lambda b, h, p: (b * nq + p, h)),
            pl.BlockSpec((t, ATTN_HEAD_DIM), lambda b, h, p: (b * nq + nq - 1 - p, h)),
            pl.BlockSpec((seq, ATTN_HEAD_DIM), lambda b, h, p: (b, N_ATTN_HEADS + h)),
            pl.BlockSpec((seq, ATTN_HEAD_DIM), lambda b, h, p: (b, 2 * N_ATTN_HEADS + h)),
            pl.BlockSpec((t, t), lambda b, h, p: (0, 0)),
        ],
        out_specs=pl.BlockSpec((None, None, 2, t, ATTN_HEAD_DIM),
                               lambda b, h, p: (b, p, 0, 0, h)),
        scratch_shapes=[pltpu.VMEM((2, t, ATTN_HEAD_DIM), BF16),
                        pltpu.VMEM((2, t, ATTN_HEAD_DIM), F32),
                        pltpu.VMEM((2, t, V7X_LANES), F32)],
        compiler_params=_params("arbitrary", "arbitrary", "arbitrary"),
        name="stickbreak_attn",
    )(proj, proj, proj, proj, upper)


def _lru_kernel(xr_ref, gr_ref, cw_ref, cb_ref, wg_ref, br_ref, bi_ref, lam_ref,
                o_ref, xbuf, hcar, a_scr, u_scr):
    ts = LRU_TS
    c = xr_ref.shape[1]
    blk = c // LRU_BLOCKS
    step = pl.program_id(1)
    halo = V7X_SUBLANES

    @pl.when(step == 0)
    def _():
        xbuf[0:halo, :] = jnp.zeros((halo, c), F32)
        hcar[...] = jnp.zeros_like(hcar)

    @pl.when(step > 0)
    def _():
        xbuf[0:halo, :] = xbuf[ts:ts + halo, :]

    xbuf[halo:ts + halo, :] = xr_ref[...].astype(F32)

    xc = cb_ref[...] + cw_ref[0:1, :] * xbuf[pl.ds(halo - (CONV_WIDTH - 1), ts), :]
    for k in range(1, CONV_WIDTH):
        xc = xc + cw_ref[k:k + 1, :] * xbuf[pl.ds(halo - (CONV_WIDTH - 1) + k, ts), :]

    sp = _softplus(-lam_ref[...])
    for n in range(LRU_BLOCKS):
        sl = slice(n * blk, (n + 1) * blk)
        xn = xc[:, sl]
        gates = jnp.dot(xn.astype(BF16), wg_ref[n], preferred_element_type=F32)
        r = jax.nn.sigmoid(gates[:, :blk] + br_ref[:, sl])
        ig = jax.nn.sigmoid(gates[:, blk:] + bi_ref[:, sl])
        log_a = (-LRU_C * r) * sp[:, sl]
        a = jnp.exp(log_a)
        a_scr[:, sl] = a
        u_scr[:, sl] = jnp.sqrt(-jnp.tanh(log_a) * (1.0 + a * a)) * (ig * xn)

    row = lax.broadcasted_iota(jnp.int32, (V7X_SUBLANES, c), 0)

    def body(g, h):
        r0 = pl.multiple_of(g * V7X_SUBLANES, V7X_SUBLANES)
        a = a_scr[pl.ds(r0, V7X_SUBLANES), :]
        u = u_scr[pl.ds(r0, V7X_SUBLANES), :]
        for sh in (1, 2, 4):
            a_prev = jnp.where(row >= sh, pltpu.roll(a, sh, 0), 1.0)
            u_prev = jnp.where(row >= sh, pltpu.roll(u, sh, 0), 0.0)
            u = a * u_prev + u
            a = a * a_prev
        hh = a * h + u
        u_scr[pl.ds(r0, V7X_SUBLANES), :] = hh
        return hh[V7X_SUBLANES - 1:V7X_SUBLANES, :]

    hcar[...] = lax.fori_loop(0, ts // V7X_SUBLANES, body, hcar[...])
    o_ref[...] = (u_scr[...] * jax.nn.gelu(gr_ref[...].astype(F32))).astype(o_ref.dtype)


def _lru(proj, xr_col, conv_w, conv_b, w_gates, b_r, b_i, lam, batch, seq):
    m = proj.shape[0]
    c = conv_w.shape[1]
    nt = seq // LRU_TS
    xr_blk = xr_col // c
    return pl.pallas_call(
        _lru_kernel,
        out_shape=jax.ShapeDtypeStruct((m, c), BF16),
        grid=(batch, nt),
        in_specs=[
            pl.BlockSpec((LRU_TS, c), lambda b, s: (b * nt + s, xr_blk)),
            pl.BlockSpec((LRU_TS, c), lambda b, s: (b * nt + s, xr_blk + 1)),
            pl.BlockSpec((CONV_WIDTH, c), lambda b, s: (0, 0)),
            pl.BlockSpec((1, c), lambda b, s: (0, 0)),
            pl.BlockSpec(w_gates.shape, lambda b, s: (0, 0, 0)),
            pl.BlockSpec((1, c), lambda b, s: (0, 0)),
            pl.BlockSpec((1, c), lambda b, s: (0, 0)),
            pl.BlockSpec((1, c), lambda b, s: (0, 0)),
        ],
        out_specs=pl.BlockSpec((LRU_TS, c), lambda b, s: (b * nt + s, 0)),
        scratch_shapes=[pltpu.VMEM((LRU_TS + V7X_SUBLANES, c), F32),
                        pltpu.VMEM((1, c), F32),
                        pltpu.VMEM((LRU_TS, c), F32),
                        pltpu.VMEM((LRU_TS, c), F32)],
        compiler_params=_params("arbitrary", "arbitrary"),
        name="rg_lru",
    )(proj, proj, conv_w, conv_b.reshape(1, c), w_gates, b_r.reshape(1, c),
      b_i.reshape(1, c), lam.reshape(1, c))


def _merge_kernel(h_ref, ya_ref, yl_ref, ga0_ref, ga1_ref, gl0_ref, gl1_ref, mod_ref,
                  wa_ref, wl_ref, wo_ref, o_ref):
    pa = jnp.dot(ya_ref[...], wa_ref[...], preferred_element_type=F32)
    plru = jnp.dot(yl_ref[...], wl_ref[...], preferred_element_type=F32)
    ga = jnp.concatenate([ga0_ref[...], ga1_ref[...]], axis=1).astype(F32)
    gl = jnp.concatenate([gl0_ref[...], gl1_ref[...]], axis=1).astype(F32)
    merged = jax.nn.sigmoid(ga) * pa + jax.nn.sigmoid(gl) * plru
    out = jnp.dot(merged.astype(BF16), wo_ref[...], preferred_element_type=F32)
    o_ref[...] = h_ref[...] + mod_ref[2:3, :] * out


def _merge(h2d, y_attn, y_lru, proj, gate_col, mod4, w_ba, w_bl, w_o, seq):
    m, d = h2d.shape
    tm = ATTN_T
    nq = seq // tm
    wa = y_attn.shape[-1]
    wl = y_lru.shape[1]
    half = d // 2
    g0 = gate_col // half
    resident = pl.Buffered(1)

    def attn_block(i):
        ii = i % nq
        first = ii < nq // 2
        return (i // nq, jnp.where(first, ii, nq - 1 - ii), jnp.where(first, 0, 1), 0, 0)

    return pl.pallas_call(
        _merge_kernel,
        out_shape=jax.ShapeDtypeStruct((m, d), F32),
        grid=(m // tm,),
        in_specs=[
            pl.BlockSpec((tm, d), lambda i: (i, 0)),
            pl.BlockSpec((None, None, None, tm, wa), attn_block),
            pl.BlockSpec((tm, wl), lambda i: (i, 0)),
            pl.BlockSpec((tm, half), lambda i: (i, g0)),
            pl.BlockSpec((tm, half), lambda i: (i, g0 + 1)),
            pl.BlockSpec((tm, half), lambda i: (i, g0 + 2)),
            pl.BlockSpec((tm, half), lambda i: (i, g0 + 3)),
            pl.BlockSpec((None, None, 3, d), lambda i: (i // nq, 1, 0, 0)),
            pl.BlockSpec((wa, d), lambda i: (0, 0), pipeline_mode=resident),
            pl.BlockSpec((wl, d), lambda i: (0, 0), pipeline_mode=resident),
            pl.BlockSpec((d, d), lambda i: (0, 0), pipeline_mode=resident),
        ],
        out_specs=pl.BlockSpec((tm, d), lambda i: (i, 0)),
        compiler_params=_params("arbitrary"),
        name="merge_outproj",
    )(h2d, y_attn, y_lru, proj, proj, proj, proj, mod4, w_ba, w_bl, w_o)


def kernel(x, c, w_ada, b_ada, norm_ffn1, w_ffn1_in, w_ffn1_out, norm_mix, w_in,
           conv_w, conv_b, w_rg_gate, b_rg_gate, w_in_gate, b_in_gate, lru_lambda,
           w_branch_attn, w_branch_lru, w_out, norm_ffn2, w_ffn2_in, w_ffn2_out,
           norm_final):
    batch, seq, d = x.shape
    depth = w_ada.shape[0]
    assert depth >= 1, "the final RMSNorm is fused into the last layer"
    attn_width = N_ATTN_HEADS * ATTN_HEAD_DIM
    lru_width = conv_w.shape[-1]
    xr_col = 3 * attn_width
    gate_col = xr_col + 2 * lru_width
    h = x.reshape(batch * seq, d)
    ones = jnp.ones((d,), F32)
    for l in range(depth):
        last = l == depth - 1
        mod = _modulation(c, w_ada[l], b_ada[l])
        mod4 = mod.reshape(batch, N_MOD // 3, 3, d)

        h = _ffn(h, mod4, 0, norm_ffn1[l], w_ffn1_in[l].astype(BF16),
                 w_ffn1_out[l].astype(BF16), ones, seq, False)

        proj = _inproj(h, mod4, norm_mix[l], w_in[l].astype(BF16), seq)
        y_attn = _attention(proj, batch, seq)
        w_gates = jnp.concatenate([w_rg_gate[l], w_in_gate[l]], axis=-1).astype(BF16)
        y_lru = _lru(proj, xr_col, conv_w[l], conv_b[l], w_gates, b_rg_gate[l],
                     b_in_gate[l], lru_lambda[l], batch, seq)
        h = _merge(h, y_attn, y_lru, proj, gate_col, mod4, w_branch_attn[l].astype(BF16),
                   w_branch_lru[l].astype(BF16), w_out[l].astype(BF16), seq)

        h = _ffn(h, mod4, 2, norm_ffn2[l], w_ffn2_in[l].astype(BF16),
                 w_ffn2_out[l].astype(BF16), norm_final if last else ones, seq, last)
    return h.reshape(batch, seq, d)
```

```python
import functools

import jax
import jax.numpy as jnp
from jax import lax
from jax.experimental import pallas as pl
from jax.experimental.pallas import tpu as pltpu

F32 = jnp.float32
BF16 = jnp.bfloat16

EPS = 1e-6
N_MOD = 9
N_ATTN_HEADS = 8
ATTN_HEAD_DIM = 128
LRU_BLOCKS = 8
CONV_WIDTH = 4
LRU_C = 8.0
LOG2E = 1.4426950408889634

V7X_LANES = 128
V7X_SUBLANES = 8
V7X_VMEM_LIMIT_BYTES = 56 * 1024 * 1024

MOD_TN = 1024
FFN_TM = 512
FFN_TF = 512
FFN_CHAINS = 2
INPROJ_TM = 1024
INPROJ_TN = 1024
ATTN_T = 256
ATTN_SKEW = (2, 3, 4)
LRU_TS = 512
NORM_UNROLL = 8
NORM_ROWS = 16


def _params(*sem):
    return pltpu.CompilerParams(dimension_semantics=sem,
                                vmem_limit_bytes=V7X_VMEM_LIMIT_BYTES)


def _softplus(x):
    return jnp.maximum(x, 0.0) + jnp.log(1.0 + jnp.exp(-jnp.abs(x)))


def _norm_modulate_rows(x_ref, gain_ref, mod_ref, y_ref):
    gain_scale = gain_ref[...] * (1.0 + mod_ref[1:2, :])
    shift = mod_ref[0:1, :]

    def body(r, carry):
        rows = pl.ds(pl.multiple_of(r * NORM_ROWS, NORM_ROWS), NORM_ROWS)
        x = x_ref[rows, :]
        y = x * lax.rsqrt(jnp.mean(x * x, axis=-1, keepdims=True) + EPS)
        y_ref[rows, :] = (y * gain_scale + shift).astype(y_ref.dtype)
        return carry

    lax.fori_loop(0, x_ref.shape[0] // NORM_ROWS, body, 0, unroll=NORM_UNROLL)


def _cast_slabs(w, nsteps):
    slab = w.size // nsteps
    rows = 2 * V7X_SUBLANES
    assert slab * nsteps == w.size and slab % (rows * V7X_LANES) == 0, w.shape
    return w.reshape(nsteps, rows, slab // rows)


def _mod_kernel(c_ref, w_ref, b_ref, o_ref):
    c = c_ref[...]
    c_act = (c * jax.nn.sigmoid(c)).astype(BF16)
    o_ref[...] = jnp.dot(c_act, w_ref[...].astype(BF16),
                         preferred_element_type=F32) + b_ref[...]


def _modulation(c, w_ada, b_ada):
    batch, d = c.shape
    n = w_ada.shape[1]
    return pl.pallas_call(
        _mod_kernel,
        out_shape=jax.ShapeDtypeStruct((batch, n), F32),
        grid=(n // MOD_TN,),
        in_specs=[pl.BlockSpec((batch, d), lambda j: (0, 0)),
                  pl.BlockSpec((d, MOD_TN), lambda j: (0, j)),
                  pl.BlockSpec((1, MOD_TN), lambda j: (0, j))],
        out_specs=pl.BlockSpec((batch, MOD_TN), lambda j: (0, j)),
        compiler_params=_params("arbitrary"),
        name="adaln_mod",
    )(c, w_ada, b_ada.reshape(1, n))


def _ffn_kernel(*refs, final_norm, n_cast):
    x_ref, mod_ref, gain_ref, wg_ref, wu_ref, wo_ref, fgain_ref = refs[:7]
    cast_in = refs[7:7 + n_cast]
    o_ref = refs[7 + n_cast]
    cast_out = refs[8 + n_cast:8 + 2 * n_cast]
    y_scr = refs[-1]
    j = pl.program_id(1)

    @pl.when(j == 0)
    def _():
        _norm_modulate_rows(x_ref, gain_ref, mod_ref, y_scr)
        o_ref[...] = jnp.zeros_like(o_ref)
        for src, dst in zip(cast_in, cast_out):
            dst[...] = src[...].astype(dst.dtype)

    y = y_scr[...]
    width = wg_ref.shape[1] // FFN_CHAINS
    part = None
    for n in range(FFN_CHAINS):
        sl = slice(n * width, (n + 1) * width)
        g = jnp.dot(y, wg_ref[:, sl], preferred_element_type=F32)
        u = jnp.dot(y, wu_ref[:, sl], preferred_element_type=F32)
        a = (g * jax.nn.sigmoid(g) * u).astype(BF16)
        down = jnp.dot(a, wo_ref[sl, :], preferred_element_type=F32)
        part = down if part is None else part + down
    o_ref[...] += part

    @pl.when(j == pl.num_programs(1) - 1)
    def _():
        h = x_ref[...] + (0.5 * mod_ref[2:3, :]) * o_ref[...]
        if final_norm:
            h = h * lax.rsqrt(jnp.mean(h * h, axis=-1, keepdims=True) + EPS)
            h = h * fgain_ref[...]
        o_ref[...] = h


def _ffn(x2d, mod4, which, gain, w_in, w_out, fgain, seq, final_norm, to_cast=()):
    m, d = x2d.shape
    f = w_out.shape[0]
    nf = f // FFN_TF
    ni = m // FFN_TM
    blocks_per_seq = seq // FFN_TM
    slabs = [_cast_slabs(w, ni) for w in to_cast]
    slab_specs = [pl.BlockSpec((None,) + s.shape[1:], lambda i, j: (i, 0, 0)) for s in slabs]
    outs = pl.pallas_call(
        functools.partial(_ffn_kernel, final_norm=final_norm, n_cast=len(slabs)),
        out_shape=[jax.ShapeDtypeStruct((m, d), F32)]
        + [jax.ShapeDtypeStruct(s.shape, BF16) for s in slabs],
        grid=(ni, nf),
        in_specs=[
            pl.BlockSpec((FFN_TM, d), lambda i, j: (i, 0)),
            pl.BlockSpec((None, None, 3, d),
                         lambda i, j: (i // blocks_per_seq, which, 0, 0)),
            pl.BlockSpec((1, d), lambda i, j: (0, 0)),
            pl.BlockSpec((d, FFN_TF), lambda i, j: (0, j)),
            pl.BlockSpec((d, FFN_TF), lambda i, j: (0, j + nf)),
            pl.BlockSpec((FFN_TF, d), lambda i, j: (j, 0)),
            pl.BlockSpec((1, d), lambda i, j: (0, 0)),
        ] + slab_specs,
        out_specs=[pl.BlockSpec((FFN_TM, d), lambda i, j: (i, 0))] + slab_specs,
        scratch_shapes=[pltpu.VMEM((FFN_TM, d), BF16)],
        compiler_params=_params("arbitrary", "arbitrary"),
        name="ffn_final" if final_norm else "ffn",
    )(x2d, mod4, gain.reshape(1, d), w_in, w_in, w_out, fgain.reshape(1, d), *slabs)
    return outs[0], [o.reshape(w.shape) for o, w in zip(outs[1:], to_cast)]


def _inproj_kernel(h_ref, mod_ref, gain_ref, w_ref, o_ref, y_scr):
    @pl.when(pl.program_id(1) == 0)
    def _():
        _norm_modulate_rows(h_ref, gain_ref, mod_ref, y_scr)

    o_ref[...] = jnp.dot(y_scr[...], w_ref[...],
                         preferred_element_type=F32).astype(o_ref.dtype)


def _inproj(h2d, mod4, gain, w_in, seq):
    m, d = h2d.shape
    n = w_in.shape[1]
    blocks_per_seq = seq // INPROJ_TM
    return pl.pallas_call(
        _inproj_kernel,
        out_shape=jax.ShapeDtypeStruct((m, n), BF16),
        grid=(m // INPROJ_TM, n // INPROJ_TN),
        in_specs=[
            pl.BlockSpec((INPROJ_TM, d), lambda i, j: (i, 0)),
            pl.BlockSpec((None, None, 3, d),
                         lambda i, j: (i // blocks_per_seq, 1, 0, 0)),
            pl.BlockSpec((1, d), lambda i, j: (0, 0)),
            pl.BlockSpec((d, INPROJ_TN), lambda i, j: (0, j)),
        ],
        out_specs=pl.BlockSpec((INPROJ_TM, INPROJ_TN), lambda i, j: (i, j)),
        scratch_shapes=[pltpu.VMEM((INPROJ_TM, d), BF16)],
        compiler_params=_params("arbitrary", "arbitrary"),
        name="mixer_inproj",
    )(h2d, mod4, gain.reshape(1, d), w_in)


def _attn_kernel(*refs, n_cast):
    qa_ref, qb_ref, k_ref, v_ref, tri_ref = refs[:5]
    o_ref = refs[5 + n_cast]
    q_scr, acc_scr, car_scr = refs[-3:]
    for src, dst in zip(refs[5:5 + n_cast], refs[6 + n_cast:6 + 2 * n_cast]):
        dst[...] = src[...].astype(dst.dtype)

    p = pl.program_id(2)
    t = ATTN_T
    nq = k_ref.shape[0] // t
    yscale = ATTN_HEAD_DIM ** -0.5 * LOG2E
    row = lax.broadcasted_iota(jnp.int32, (t, t), 0)
    col = lax.broadcasted_iota(jnp.int32, (t, t), 1)
    before = col < row

    q_scr[0] = (qa_ref[...].astype(F32) * yscale).astype(BF16)
    q_scr[1] = (qb_ref[...].astype(F32) * yscale).astype(BF16)

    blocks = [(0, p, True), (1, nq - 1 - p, True)]
    for s in range(nq - 1):
        blocks.append((jnp.where(s >= p, 1, 0),
                       jnp.where(s >= p, nq - 2 - s, p - 1 - s), False))

    def scores(blk):
        side, kblk, _ = blk
        k0 = pl.multiple_of(kblk * t, t)
        return lax.dot_general(q_scr[side], k_ref[pl.ds(k0, t), :],
                               (((1,), (1,)), ((), ())), preferred_element_type=F32)

    def log_keep(blk, y):
        pos = jnp.maximum(y, 0.0) + jnp.log(1.0 + jnp.exp2(-jnp.abs(y))) * LOG2E
        if blk[2]:
            pos = jnp.where(before, pos, 0.0)
        return y, jnp.dot(pos.astype(BF16), tri_ref[...], preferred_element_type=F32)

    def weights(blk, y, csum):
        side, kblk, diag = blk
        total = jnp.broadcast_to(csum[:, 0:1], (t, V7X_LANES))
        if diag:
            car_scr[side] = total
            w = jnp.where(before, jnp.exp2(y - csum), 0.0)
        else:
            carry = car_scr[side]
            car_scr[side] = carry + total
            w = jnp.exp2(y - csum - jnp.concatenate([carry, carry], axis=1))
        k0 = pl.multiple_of(kblk * t, t)
        return jnp.dot(w.astype(BF16), v_ref[pl.ds(k0, t), :], preferred_element_type=F32)

    def accumulate(blk, pv):
        if blk[2]:
            acc_scr[blk[0]] = pv
        else:
            acc_scr[blk[0]] += pv

    nb = len(blocks)
    raw, ycs, pvs = {}, {}, {}
    d1, d2, d3 = ATTN_SKEW
    for step in range(nb + d3):
        if step < nb:
            raw[step] = scores(blocks[step])
        if 0 <= step - d1 < nb:
            ycs[step - d1] = log_keep(blocks[step - d1], raw.pop(step - d1))
        if 0 <= step - d2 < nb:
            pvs[step - d2] = weights(blocks[step - d2], *ycs.pop(step - d2))
        if 0 <= step - d3 < nb:
            accumulate(blocks[step - d3], pvs.pop(step - d3))

    o_ref[...] = acc_scr[...].astype(o_ref.dtype)


def _attention(proj, batch, seq, to_cast=()):
    t = ATTN_T
    nq = seq // t
    half = nq // 2
    width = N_ATTN_HEADS * ATTN_HEAD_DIM
    idx = jnp.arange(t)
    upper = (idx[:, None] >= idx[None, :]).astype(BF16)
    slabs = [_cast_slabs(w, batch * N_ATTN_HEADS * half) for w in to_cast]
    slab_specs = [pl.BlockSpec((None,) + s.shape[1:],
                               lambda b, h, p: ((b * N_ATTN_HEADS + h) * half + p, 0, 0))
                  for s in slabs]
    outs = pl.pallas_call(
        functools.partial(_attn_kernel, n_cast=len(slabs)),
        out_shape=[jax.ShapeDtypeStruct((batch, half, 2, t, width), BF16)]
        + [jax.ShapeDtypeStruct(s.shape, BF16) for s in slabs],
        grid=(batch, N_ATTN_HEADS, half),
        in_specs=[
            pl.BlockSpec((t, ATTN_HEAD_DIM), lambda b, h, p: (b * nq + p, h)),
            pl.BlockSpec((t, ATTN_HEAD_DIM), lambda b, h, p: (b * nq + nq - 1 - p, h)),
            pl.BlockSpec((seq, ATTN_HEAD_DIM), lambda b, h, p: (b, N_ATTN_HEADS + h)),
            pl.BlockSpec((seq, ATTN_HEAD_DIM), lambda b, h, p: (b, 2 * N_ATTN_HEADS + h)),
            pl.BlockSpec((t, t), lambda b, h, p: (0, 0)),
        ] + slab_specs,
        out_specs=[pl.BlockSpec((None, None, 2, t, ATTN_HEAD_DIM),
                                lambda b, h, p: (b, p, 0, 0, h))] + slab_specs,
        scratch_shapes=[pltpu.VMEM((2, t, ATTN_HEAD_DIM), BF16),
                        pltpu.VMEM((2, t, ATTN_HEAD_DIM), F32),
                        pltpu.VMEM((2, t, V7X_LANES), F32)],
        compiler_params=_params("arbitrary", "arbitrary", "arbitrary"),
        name="stickbreak_attn",
    )(proj, proj, proj, proj, upper, *slabs)
    return outs[0], [o.reshape(w.shape) for o, w in zip(outs[1:], to_cast)]


def _lru_kernel(xr_ref, gr_ref, cw_ref, cb_ref, wg_ref, br_ref, bi_ref, lam_ref,
                o_ref, xbuf, hcar, a_scr, u_scr):
    ts = LRU_TS
    c = xr_ref.shape[1]
    blk = c // LRU_BLOCKS
    step = pl.program_id(1)
    halo = V7X_SUBLANES

    @pl.when(step == 0)
    def _():
        xbuf[0:halo, :] = jnp.zeros((halo, c), F32)
        hcar[...] = jnp.zeros_like(hcar)

    @pl.when(step > 0)
    def _():
        xbuf[0:halo, :] = xbuf[ts:ts + halo, :]

    xbuf[halo:ts + halo, :] = xr_ref[...].astype(F32)

    xc = cb_ref[...] + cw_ref[0:1, :] * xbuf[pl.ds(halo - (CONV_WIDTH - 1), ts), :]
    for k in range(1, CONV_WIDTH):
        xc = xc + cw_ref[k:k + 1, :] * xbuf[pl.ds(halo - (CONV_WIDTH - 1) + k, ts), :]

    sp = _softplus(-lam_ref[...])
    for n in range(LRU_BLOCKS):
        sl = slice(n * blk, (n + 1) * blk)
        xn = xc[:, sl]
        gates = jnp.dot(xn.astype(BF16), wg_ref[n], preferred_element_type=F32)
        r = jax.nn.sigmoid(gates[:, :blk] + br_ref[:, sl])
        ig = jax.nn.sigmoid(gates[:, blk:] + bi_ref[:, sl])
        log_a = (-LRU_C * r) * sp[:, sl]
        a = jnp.exp(log_a)
        a_scr[:, sl] = a
        u_scr[:, sl] = jnp.sqrt(-jnp.tanh(log_a) * (1.0 + a * a)) * (ig * xn)

    row = lax.broadcasted_iota(jnp.int32, (V7X_SUBLANES, c), 0)

    def body(g, h):
        r0 = pl.multiple_of(g * V7X_SUBLANES, V7X_SUBLANES)
        a = a_scr[pl.ds(r0, V7X_SUBLANES), :]
        u = u_scr[pl.ds(r0, V7X_SUBLANES), :]
        for sh in (1, 2, 4):
            a_prev = jnp.where(row >= sh, pltpu.roll(a, sh, 0), 1.0)
            u_prev = jnp.where(row >= sh, pltpu.roll(u, sh, 0), 0.0)
            u = a * u_prev + u
            a = a * a_prev
        hh = a * h + u
        u_scr[pl.ds(r0, V7X_SUBLANES), :] = hh
        return hh[V7X_SUBLANES - 1:V7X_SUBLANES, :]

    hcar[...] = lax.fori_loop(0, ts // V7X_SUBLANES, body, hcar[...])
    o_ref[...] = (u_scr[...] * jax.nn.gelu(gr_ref[...].astype(F32))).astype(o_ref.dtype)


def _lru(proj, xr_col, conv_w, conv_b, w_gates, b_r, b_i, lam, batch, seq):
    m = proj.shape[0]
    c = conv_w.shape[1]
    nt = seq // LRU_TS
    xr_blk = xr_col // c
    return pl.pallas_call(
        _lru_kernel,
        out_shape=jax.ShapeDtypeStruct((m, c), BF16),
        grid=(batch, nt),
        in_specs=[
            pl.BlockSpec((LRU_TS, c), lambda b, s: (b * nt + s, xr_blk)),
            pl.BlockSpec((LRU_TS, c), lambda b, s: (b * nt + s, xr_blk + 1)),
            pl.BlockSpec((CONV_WIDTH, c), lambda b, s: (0, 0)),
            pl.BlockSpec((1, c), lambda b, s: (0, 0)),
            pl.BlockSpec(w_gates.shape, lambda b, s: (0, 0, 0)),
            pl.BlockSpec((1, c), lambda b, s: (0, 0)),
            pl.BlockSpec((1, c), lambda b, s: (0, 0)),
            pl.BlockSpec((1, c), lambda b, s: (0, 0)),
        ],
        out_specs=pl.BlockSpec((LRU_TS, c), lambda b, s: (b * nt + s, 0)),
        scratch_shapes=[pltpu.VMEM((LRU_TS + V7X_SUBLANES, c), F32),
                        pltpu.VMEM((1, c), F32),
                        pltpu.VMEM((LRU_TS, c), F32),
                        pltpu.VMEM((LRU_TS, c), F32)],
        compiler_params=_params("arbitrary", "arbitrary"),
        name="rg_lru",
    )(proj, proj, conv_w, conv_b.reshape(1, c), w_gates, b_r.reshape(1, c),
      b_i.reshape(1, c), lam.reshape(1, c))


def _merge_kernel(h_ref, ya_ref, yl_ref, ga0_ref, ga1_ref, gl0_ref, gl1_ref, mod_ref,
                  wa_ref, wl_ref, wo_ref, o_ref):
    pa = jnp.dot(ya_ref[...], wa_ref[...], preferred_element_type=F32)
    plru = jnp.dot(yl_ref[...], wl_ref[...], preferred_element_type=F32)
    ga = jnp.concatenate([ga0_ref[...], ga1_ref[...]], axis=1).astype(F32)
    gl = jnp.concatenate([gl0_ref[...], gl1_ref[...]], axis=1).astype(F32)
    merged = jax.nn.sigmoid(ga) * pa + jax.nn.sigmoid(gl) * plru
    out = jnp.dot(merged.astype(BF16), wo_ref[...], preferred_element_type=F32)
    o_ref[...] = h_ref[...] + mod_ref[2:3, :] * out


def _merge(h2d, y_attn, y_lru, proj, gate_col, mod4, w_ba, w_bl, w_o, seq):
    m, d = h2d.shape
    tm = ATTN_T
    nq = seq // tm
    wa = y_attn.shape[-1]
    wl = y_lru.shape[1]
    half = d // 2
    g0 = gate_col // half
    resident = pl.Buffered(1)

    def attn_block(i):
        ii = i % nq
        first = ii < nq // 2
        return (i // nq, jnp.where(first, ii, nq - 1 - ii), jnp.where(first, 0, 1), 0, 0)

    return pl.pallas_call(
        _merge_kernel,
        out_shape=jax.ShapeDtypeStruct((m, d), F32),
        grid=(m // tm,),
        in_specs=[
            pl.BlockSpec((tm, d), lambda i: (i, 0)),
            pl.BlockSpec((None, None, None, tm, wa), attn_block),
            pl.BlockSpec((tm, wl), lambda i: (i, 0)),
            pl.BlockSpec((tm, half), lambda i: (i, g0)),
            pl.BlockSpec((tm, half), lambda i: (i, g0 + 1)),
            pl.BlockSpec((tm, half), lambda i: (i, g0 + 2)),
            pl.BlockSpec((tm, half), lambda i: (i, g0 + 3)),
            pl.BlockSpec((None, None, 3, d), lambda i: (i // nq, 1, 0, 0)),
            pl.BlockSpec((wa, d), lambda i: (0, 0), pipeline_mode=resident),
            pl.BlockSpec((wl, d), lambda i: (0, 0), pipeline_mode=resident),
            pl.BlockSpec((d, d), lambda i: (0, 0), pipeline_mode=resident),
        ],
        out_specs=pl.BlockSpec((tm, d), lambda i: (i, 0)),
        compiler_params=_params("arbitrary"),
        name="merge_outproj",
    )(h2d, y_attn, y_lru, proj, proj, proj, proj, mod4, w_ba, w_bl, w_o)


def kernel(x, c, w_ada, b_ada, norm_ffn1, w_ffn1_in, w_ffn1_out, norm_mix, w_in,
           conv_w, conv_b, w_rg_gate, b_rg_gate, w_in_gate, b_in_gate, lru_lambda,
           w_branch_attn, w_branch_lru, w_out, norm_ffn2, w_ffn2_in, w_ffn2_out,
           norm_final):
    batch, seq, d = x.shape
    depth = w_ada.shape[0]
    assert depth >= 1, "the final RMSNorm is fused into the last layer"
    attn_width = N_ATTN_HEADS * ATTN_HEAD_DIM
    lru_width = conv_w.shape[-1]
    xr_col = 3 * attn_width
    gate_col = xr_col + 2 * lru_width
    h = x.reshape(batch * seq, d)
    ones = jnp.ones((d,), F32)
    for l in range(depth):
        last = l == depth - 1
        mod = _modulation(c, w_ada[l], b_ada[l])
        mod4 = mod.reshape(batch, N_MOD // 3, 3, d)

        h, (w_in_bf,) = _ffn(h, mod4, 0, norm_ffn1[l], w_ffn1_in[l].astype(BF16),
                             w_ffn1_out[l].astype(BF16), ones, seq, False, to_cast=(w_in[l],))

        proj = _inproj(h, mod4, norm_mix[l], w_in_bf, seq)
        y_attn, (w_ba, w_bl, w_o, w_f2_in, w_f2_out) = _attention(
            proj, batch, seq,
            to_cast=(w_branch_attn[l], w_branch_lru[l], w_out[l], w_ffn2_in[l], w_ffn2_out[l]))
        w_gates = jnp.concatenate([w_rg_gate[l], w_in_gate[l]], axis=-1).astype(BF16)
        y_lru = _lru(proj, xr_col, conv_w[l], conv_b[l], w_gates, b_rg_gate[l],
                     b_in_gate[l], lru_lambda[l], batch, seq)
        h = _merge(h, y_attn, y_lru, proj, gate_col, mod4, w_ba, w_bl, w_o, seq)

        h, _ = _ffn(h, mod4, 2, norm_ffn2[l], w_f2_in, w_f2_out,
                    norm_final if last else ones, seq, last)
    return h.reshape(batch, seq, d)
```

```python
import functools

import jax
import jax.numpy as jnp
from jax import lax
from jax.experimental import pallas as pl
from jax.experimental.pallas import tpu as pltpu

F32 = jnp.float32
BF16 = jnp.bfloat16

EPS = 1e-6
N_MOD = 9
N_ATTN_HEADS = 8
ATTN_HEAD_DIM = 128
LRU_BLOCKS = 8
CONV_WIDTH = 4
LRU_C = 8.0
LOG2E = 1.4426950408889634

V7X_LANES = 128
V7X_SUBLANES = 8
V7X_VMEM_LIMIT_BYTES = 56 * 1024 * 1024

MOD_TN = 1024
FFN_TM = 512
FFN_TF = 512
FFN_CHAINS = 2
INPROJ_TM = 1024
INPROJ_TN = 1024
ATTN_T = 256
ATTN_SKEW = (2, 3, 4)
LRU_TS = 512
NORM_UNROLL = 8
NORM_ROWS = 16


def _params(*sem):
    return pltpu.CompilerParams(dimension_semantics=sem,
                                vmem_limit_bytes=V7X_VMEM_LIMIT_BYTES)


def _softplus(x):
    return jnp.maximum(x, 0.0) + jnp.log(1.0 + jnp.exp(-jnp.abs(x)))


def _norm_modulate_rows(x_ref, gain_ref, mod_ref, y_ref):
    gain_scale = gain_ref[...] * (1.0 + mod_ref[1:2, :])
    shift = mod_ref[0:1, :]

    def body(r, carry):
        rows = pl.ds(pl.multiple_of(r * NORM_ROWS, NORM_ROWS), NORM_ROWS)
        x = x_ref[rows, :]
        y = x * lax.rsqrt(jnp.mean(x * x, axis=-1, keepdims=True) + EPS)
        y_ref[rows, :] = (y * gain_scale + shift).astype(y_ref.dtype)
        return carry

    lax.fori_loop(0, x_ref.shape[0] // NORM_ROWS, body, 0, unroll=NORM_UNROLL)


def _cast_block(shape, nsteps):
    r, c = shape
    row_tile = 2 * V7X_SUBLANES
    best = None
    for row_blocks in range(1, nsteps + 1):
        if nsteps % row_blocks or r % row_blocks or (r // row_blocks) % row_tile:
            continue
        col_blocks = nsteps // row_blocks
        if c % col_blocks or (c // col_blocks) % V7X_LANES:
            continue
        if best is None or c // col_blocks > best[1]:
            best = (r // row_blocks, c // col_blocks)
    assert best is not None, (shape, nsteps)
    return best


def _cast_specs(to_cast, nsteps, step_of):
    in_specs, out_specs, out_shapes = [], [], []
    for w, layer in to_cast:
        rows, cols = _cast_block(w.shape[1:], nsteps)
        col_blocks = w.shape[2] // cols

        def in_map(*idx, layer=layer, col_blocks=col_blocks):
            s = step_of(*idx)
            return (layer, s // col_blocks, s % col_blocks)

        def out_map(*idx, col_blocks=col_blocks):
            s = step_of(*idx)
            return (s // col_blocks, s % col_blocks)

        in_specs.append(pl.BlockSpec((None, rows, cols), in_map))
        out_specs.append(pl.BlockSpec((rows, cols), out_map))
        out_shapes.append(jax.ShapeDtypeStruct(w.shape[1:], BF16))
    return in_specs, out_specs, out_shapes


def _mod_kernel(c_ref, w_ref, b_ref, o_ref):
    c = c_ref[...]
    c_act = (c * jax.nn.sigmoid(c)).astype(BF16)
    o_ref[...] = jnp.dot(c_act, w_ref[...].astype(BF16),
                         preferred_element_type=F32) + b_ref[...]


def _modulation(c, w_ada, b_ada):
    batch, d = c.shape
    n = w_ada.shape[1]
    return pl.pallas_call(
        _mod_kernel,
        out_shape=jax.ShapeDtypeStruct((batch, n), F32),
        grid=(n // MOD_TN,),
        in_specs=[pl.BlockSpec((batch, d), lambda j: (0, 0)),
                  pl.BlockSpec((d, MOD_TN), lambda j: (0, j)),
                  pl.BlockSpec((1, MOD_TN), lambda j: (0, j))],
        out_specs=pl.BlockSpec((batch, MOD_TN), lambda j: (0, j)),
        compiler_params=_params("arbitrary"),
        name="adaln_mod",
    )(c, w_ada, b_ada.reshape(1, n))


def _ffn_kernel(*refs, final_norm, n_cast):
    x_ref, mod_ref, gain_ref, wg_ref, wu_ref, wo_ref, fgain_ref = refs[:7]
    cast_in = refs[7:7 + n_cast]
    o_ref = refs[7 + n_cast]
    cast_out = refs[8 + n_cast:8 + 2 * n_cast]
    y_scr = refs[-1]
    j = pl.program_id(1)

    @pl.when(j == 0)
    def _():
        _norm_modulate_rows(x_ref, gain_ref, mod_ref, y_scr)
        o_ref[...] = jnp.zeros_like(o_ref)
        for src, dst in zip(cast_in, cast_out):
            dst[...] = src[...].astype(dst.dtype)

    y = y_scr[...]
    width = wg_ref.shape[1] // FFN_CHAINS
    part = None
    for n in range(FFN_CHAINS):
        sl = slice(n * width, (n + 1) * width)
        g = jnp.dot(y, wg_ref[:, sl], preferred_element_type=F32)
        u = jnp.dot(y, wu_ref[:, sl], preferred_element_type=F32)
        a = (g * jax.nn.sigmoid(g) * u).astype(BF16)
        down = jnp.dot(a, wo_ref[sl, :], preferred_element_type=F32)
        part = down if part is None else part + down
    o_ref[...] += part

    @pl.when(j == pl.num_programs(1) - 1)
    def _():
        h = x_ref[...] + (0.5 * mod_ref[2:3, :]) * o_ref[...]
        if final_norm:
            h = h * lax.rsqrt(jnp.mean(h * h, axis=-1, keepdims=True) + EPS)
            h = h * fgain_ref[...]
        o_ref[...] = h


def _ffn(x2d, mod4, which, gain, w_in, w_out, fgain, seq, final_norm, to_cast=()):
    m, d = x2d.shape
    f = w_out.shape[0]
    nf = f // FFN_TF
    ni = m // FFN_TM
    blocks_per_seq = seq // FFN_TM
    cast_in, cast_out, cast_shapes = _cast_specs(to_cast, ni, lambda i, j: i)
    outs = pl.pallas_call(
        functools.partial(_ffn_kernel, final_norm=final_norm, n_cast=len(to_cast)),
        out_shape=[jax.ShapeDtypeStruct((m, d), F32)] + cast_shapes,
        grid=(ni, nf),
        in_specs=[
            pl.BlockSpec((FFN_TM, d), lambda i, j: (i, 0)),
            pl.BlockSpec((None, None, 3, d),
                         lambda i, j: (i // blocks_per_seq, which, 0, 0)),
            pl.BlockSpec((1, d), lambda i, j: (0, 0)),
            pl.BlockSpec((d, FFN_TF), lambda i, j: (0, j)),
            pl.BlockSpec((d, FFN_TF), lambda i, j: (0, j + nf)),
            pl.BlockSpec((FFN_TF, d), lambda i, j: (j, 0)),
            pl.BlockSpec((1, d), lambda i, j: (0, 0)),
        ] + cast_in,
        out_specs=[pl.BlockSpec((FFN_TM, d), lambda i, j: (i, 0))] + cast_out,
        scratch_shapes=[pltpu.VMEM((FFN_TM, d), BF16)],
        compiler_params=_params("arbitrary", "arbitrary"),
        name="ffn_final" if final_norm else "ffn",
    )(x2d, mod4, gain.reshape(1, d), w_in, w_in, w_out, fgain.reshape(1, d),
      *[w for w, _ in to_cast])
    return outs[0], outs[1:]


def _inproj_kernel(h_ref, mod_ref, gain_ref, w_ref, o_ref, y_scr):
    @pl.when(pl.program_id(1) == 0)
    def _():
        _norm_modulate_rows(h_ref, gain_ref, mod_ref, y_scr)

    o_ref[...] = jnp.dot(y_scr[...], w_ref[...],
                         preferred_element_type=F32).astype(o_ref.dtype)


def _inproj(h2d, mod4, gain, w_in, seq):
    m, d = h2d.shape
    n = w_in.shape[1]
    blocks_per_seq = seq // INPROJ_TM
    return pl.pallas_call(
        _inproj_kernel,
        out_shape=jax.ShapeDtypeStruct((m, n), BF16),
        grid=(m // INPROJ_TM, n // INPROJ_TN),
        in_specs=[
            pl.BlockSpec((INPROJ_TM, d), lambda i, j: (i, 0)),
            pl.BlockSpec((None, None, 3, d),
                         lambda i, j: (i // blocks_per_seq, 1, 0, 0)),
            pl.BlockSpec((1, d), lambda i, j: (0, 0)),
            pl.BlockSpec((d, INPROJ_TN), lambda i, j: (0, j)),
        ],
        out_specs=pl.BlockSpec((INPROJ_TM, INPROJ_TN), lambda i, j: (i, j)),
        scratch_shapes=[pltpu.VMEM((INPROJ_TM, d), BF16)],
        compiler_params=_params("arbitrary", "arbitrary"),
        name="mixer_inproj",
    )(h2d, mod4, gain.reshape(1, d), w_in)


def _attn_kernel(*refs, n_cast):
    qa_ref, qb_ref, k_ref, v_ref, tri_ref = refs[:5]
    o_ref = refs[5 + n_cast]
    q_scr, acc_scr, car_scr = refs[-3:]
    for src, dst in zip(refs[5:5 + n_cast], refs[6 + n_cast:6 + 2 * n_cast]):
        dst[...] = src[...].astype(dst.dtype)

    p = pl.program_id(2)
    t = ATTN_T
    nq = k_ref.shape[0] // t
    yscale = ATTN_HEAD_DIM ** -0.5 * LOG2E
    row = lax.broadcasted_iota(jnp.int32, (t, t), 0)
    col = lax.broadcasted_iota(jnp.int32, (t, t), 1)
    before = col < row

    q_scr[0] = (qa_ref[...].astype(F32) * yscale).astype(BF16)
    q_scr[1] = (qb_ref[...].astype(F32) * yscale).astype(BF16)

    blocks = [(0, p, True), (1, nq - 1 - p, True)]
    for s in range(nq - 1):
        blocks.append((jnp.where(s >= p, 1, 0),
                       jnp.where(s >= p, nq - 2 - s, p - 1 - s), False))

    def scores(blk):
        side, kblk, _ = blk
        k0 = pl.multiple_of(kblk * t, t)
        return lax.dot_general(q_scr[side], k_ref[pl.ds(k0, t), :],
                               (((1,), (1,)), ((), ())), preferred_element_type=F32)

    def log_keep(blk, y):
        pos = jnp.maximum(y, 0.0) + jnp.log(1.0 + jnp.exp2(-jnp.abs(y))) * LOG2E
        if blk[2]:
            pos = jnp.where(before, pos, 0.0)
        return y, jnp.dot(pos.astype(BF16), tri_ref[...], preferred_element_type=F32)

    def weights(blk, y, csum):
        side, kblk, diag = blk
        total = jnp.broadcast_to(csum[:, 0:1], (t, V7X_LANES))
        if diag:
            car_scr[side] = total
            w = jnp.where(before, jnp.exp2(y - csum), 0.0)
        else:
            carry = car_scr[side]
            car_scr[side] = carry + total
            w = jnp.exp2(y - csum - jnp.concatenate([carry, carry], axis=1))
        k0 = pl.multiple_of(kblk * t, t)
        return jnp.dot(w.astype(BF16), v_ref[pl.ds(k0, t), :], preferred_element_type=F32)

    def accumulate(blk, pv):
        if blk[2]:
            acc_scr[blk[0]] = pv
        else:
            acc_scr[blk[0]] += pv

    nb = len(blocks)
    raw, ycs, pvs = {}, {}, {}
    d1, d2, d3 = ATTN_SKEW
    for step in range(nb + d3):
        if step < nb:
            raw[step] = scores(blocks[step])
        if 0 <= step - d1 < nb:
            ycs[step - d1] = log_keep(blocks[step - d1], raw.pop(step - d1))
        if 0 <= step - d2 < nb:
            pvs[step - d2] = weights(blocks[step - d2], *ycs.pop(step - d2))
        if 0 <= step - d3 < nb:
            accumulate(blocks[step - d3], pvs.pop(step - d3))

    o_ref[...] = acc_scr[...].astype(o_ref.dtype)


def _attention(proj, batch, seq, to_cast=()):
    t = ATTN_T
    nq = seq // t
    half = nq // 2
    width = N_ATTN_HEADS * ATTN_HEAD_DIM
    idx = jnp.arange(t)
    upper = (idx[:, None] >= idx[None, :]).astype(BF16)
    cast_in, cast_out, cast_shapes = _cast_specs(
        to_cast, batch * N_ATTN_HEADS * half,
        lambda b, h, p: (b * N_ATTN_HEADS + h) * half + p)
    outs = pl.pallas_call(
        functools.partial(_attn_kernel, n_cast=len(to_cast)),
        out_shape=[jax.ShapeDtypeStruct((batch, half, 2, t, width), BF16)] + cast_shapes,
        grid=(batch, N_ATTN_HEADS, half),
        in_specs=[
            pl.BlockSpec((t, ATTN_HEAD_DIM), lambda b, h, p: (b * nq + p, h)),
            pl.BlockSpec((t, ATTN_HEAD_DIM), lambda b, h, p: (b * nq + nq - 1 - p, h)),
            pl.BlockSpec((seq, ATTN_HEAD_DIM), lambda b, h, p: (b, N_ATTN_HEADS + h)),
            pl.BlockSpec((seq, ATTN_HEAD_DIM), lambda b, h, p: (b, 2 * N_ATTN_HEADS + h)),
            pl.BlockSpec((t, t), lambda b, h, p: (0, 0)),
        ] + cast_in,
        out_specs=[pl.BlockSpec((None, None, 2, t, ATTN_HEAD_DIM),
                                lambda b, h, p: (b, p, 0, 0, h))] + cast_out,
        scratch_shapes=[pltpu.VMEM((2, t, ATTN_HEAD_DIM), BF16),
                        pltpu.VMEM((2, t, ATTN_HEAD_DIM), F32),
                        pltpu.VMEM((2, t, V7X_LANES), F32)],
        compiler_params=_params("arbitrary", "arbitrary", "arbitrary"),
        name="stickbreak_attn",
    )(proj, proj, proj, proj, upper, *[w for w, _ in to_cast])
    return outs[0], outs[1:]


def _lru_kernel(xr_ref, gr_ref, cw_ref, cb_ref, wg_ref, br_ref, bi_ref, lam_ref,
                o_ref, xbuf, hcar, a_scr, u_scr):
    ts = LRU_TS
    c = xr_ref.shape[1]
    blk = c // LRU_BLOCKS
    step = pl.program_id(1)
    halo = V7X_SUBLANES

    @pl.when(step == 0)
    def _():
        xbuf[0:halo, :] = jnp.zeros((halo, c), F32)
        hcar[...] = jnp.zeros_like(hcar)

    @pl.when(step > 0)
    def _():
        xbuf[0:halo, :] = xbuf[ts:ts + halo, :]

    xbuf[halo:ts + halo, :] = xr_ref[...].astype(F32)

    xc = cb_ref[...] + cw_ref[0:1, :] * xbuf[pl.ds(halo - (CONV_WIDTH - 1), ts), :]
    for k in range(1, CONV_WIDTH):
        xc = xc + cw_ref[k:k + 1, :] * xbuf[pl.ds(halo - (CONV_WIDTH - 1) + k, ts), :]

    sp = _softplus(-lam_ref[...])
    for n in range(LRU_BLOCKS):
        sl = slice(n * blk, (n + 1) * blk)
        xn = xc[:, sl]
        gates = jnp.dot(xn.astype(BF16), wg_ref[n], preferred_element_type=F32)
        r = jax.nn.sigmoid(gates[:, :blk] + br_ref[:, sl])
        ig = jax.nn.sigmoid(gates[:, blk:] + bi_ref[:, sl])
        log_a = (-LRU_C * r) * sp[:, sl]
        a = jnp.exp(log_a)
        a_scr[:, sl] = a
        u_scr[:, sl] = jnp.sqrt(-jnp.tanh(log_a) * (1.0 + a * a)) * (ig * xn)

    row = lax.broadcasted_iota(jnp.int32, (V7X_SUBLANES, c), 0)

    def body(g, h):
        r0 = pl.multiple_of(g * V7X_SUBLANES, V7X_SUBLANES)
        a = a_scr[pl.ds(r0, V7X_SUBLANES), :]
        u = u_scr[pl.ds(r0, V7X_SUBLANES), :]
        for sh in (1, 2, 4):
            a_prev = jnp.where(row >= sh, pltpu.roll(a, sh, 0), 1.0)
            u_prev = jnp.where(row >= sh, pltpu.roll(u, sh, 0), 0.0)
            u = a * u_prev + u
            a = a * a_prev
        hh = a * h + u
        u_scr[pl.ds(r0, V7X_SUBLANES), :] = hh
        return hh[V7X_SUBLANES - 1:V7X_SUBLANES, :]

    hcar[...] = lax.fori_loop(0, ts // V7X_SUBLANES, body, hcar[...])
    o_ref[...] = (u_scr[...] * jax.nn.gelu(gr_ref[...].astype(F32))).astype(o_ref.dtype)


def _lru(proj, xr_col, conv_w, conv_b, w_gates, b_r, b_i, lam, batch, seq):
    m = proj.shape[0]
    c = conv_w.shape[1]
    nt = seq // LRU_TS
    xr_blk = xr_col // c
    return pl.pallas_call(
        _lru_kernel,
        out_shape=jax.ShapeDtypeStruct((m, c), BF16),
        grid=(batch, nt),
        in_specs=[
            pl.BlockSpec((LRU_TS, c), lambda b, s: (b * nt + s, xr_blk)),
            pl.BlockSpec((LRU_TS, c), lambda b, s: (b * nt + s, xr_blk + 1)),
            pl.BlockSpec((CONV_WIDTH, c), lambda b, s: (0, 0)),
            pl.BlockSpec((1, c), lambda b, s: (0, 0)),
            pl.BlockSpec(w_gates.shape, lambda b, s: (0, 0, 0)),
            pl.BlockSpec((1, c), lambda b, s: (0, 0)),
            pl.BlockSpec((1, c), lambda b, s: (0, 0)),
            pl.BlockSpec((1, c), lambda b, s: (0, 0)),
        ],
        out_specs=pl.BlockSpec((LRU_TS, c), lambda b, s: (b * nt + s, 0)),
        scratch_shapes=[pltpu.VMEM((LRU_TS + V7X_SUBLANES, c), F32),
                        pltpu.VMEM((1, c), F32),
                        pltpu.VMEM((LRU_TS, c), F32),
                        pltpu.VMEM((LRU_TS, c), F32)],
        compiler_params=_params("arbitrary", "arbitrary"),
        name="rg_lru",
    )(proj, proj, conv_w, conv_b.reshape(1, c), w_gates, b_r.reshape(1, c),
      b_i.reshape(1, c), lam.reshape(1, c))


def _merge_kernel(h_ref, ya_ref, yl_ref, ga0_ref, ga1_ref, gl0_ref, gl1_ref, mod_ref,
                  wa_ref, wl_ref, wo_ref, o_ref):
    pa = jnp.dot(ya_ref[...], wa_ref[...], preferred_element_type=F32)
    plru = jnp.dot(yl_ref[...], wl_ref[...], preferred_element_type=F32)
    ga = jnp.concatenate([ga0_ref[...], ga1_ref[...]], axis=1).astype(F32)
    gl = jnp.concatenate([gl0_ref[...], gl1_ref[...]], axis=1).astype(F32)
    merged = jax.nn.sigmoid(ga) * pa + jax.nn.sigmoid(gl) * plru
    out = jnp.dot(merged.astype(BF16), wo_ref[...], preferred_element_type=F32)
    o_ref[...] = h_ref[...] + mod_ref[2:3, :] * out


def _merge(h2d, y_attn, y_lru, proj, gate_col, mod4, w_ba, w_bl, w_o, seq):
    m, d = h2d.shape
    tm = ATTN_T
    nq = seq // tm
    wa = y_attn.shape[-1]
    wl = y_lru.shape[1]
    half = d // 2
    g0 = gate_col // half
    resident = pl.Buffered(1)

    def attn_block(i):
        ii = i % nq
        first = ii < nq // 2
        return (i // nq, jnp.where(first, ii, nq - 1 - ii), jnp.where(first, 0, 1), 0, 0)

    return pl.pallas_call(
        _merge_kernel,
        out_shape=jax.ShapeDtypeStruct((m, d), F32),
        grid=(m // tm,),
        in_specs=[
            pl.BlockSpec((tm, d), lambda i: (i, 0)),
            pl.BlockSpec((None, None, None, tm, wa), attn_block),
            pl.BlockSpec((tm, wl), lambda i: (i, 0)),
            pl.BlockSpec((tm, half), lambda i: (i, g0)),
            pl.BlockSpec((tm, half), lambda i: (i, g0 + 1)),
            pl.BlockSpec((tm, half), lambda i: (i, g0 + 2)),
            pl.BlockSpec((tm, half), lambda i: (i, g0 + 3)),
            pl.BlockSpec((None, None, 3, d), lambda i: (i // nq, 1, 0, 0)),
            pl.BlockSpec((wa, d), lambda i: (0, 0), pipeline_mode=resident),
            pl.BlockSpec((wl, d), lambda i: (0, 0), pipeline_mode=resident),
            pl.BlockSpec((d, d), lambda i: (0, 0), pipeline_mode=resident),
        ],
        out_specs=pl.BlockSpec((tm, d), lambda i: (i, 0)),
        compiler_params=_params("arbitrary"),
        name="merge_outproj",
    )(h2d, y_attn, y_lru, proj, proj, proj, proj, mod4, w_ba, w_bl, w_o)


def kernel(x, c, w_ada, b_ada, norm_ffn1, w_ffn1_in, w_ffn1_out, norm_mix, w_in,
           conv_w, conv_b, w_rg_gate, b_rg_gate, w_in_gate, b_in_gate, lru_lambda,
           w_branch_attn, w_branch_lru, w_out, norm_ffn2, w_ffn2_in, w_ffn2_out,
           norm_final):
    batch, seq, d = x.shape
    depth = w_ada.shape[0]
    assert depth >= 1, "the final RMSNorm is fused into the last layer"
    attn_width = N_ATTN_HEADS * ATTN_HEAD_DIM
    lru_width = conv_w.shape[-1]
    xr_col = 3 * attn_width
    gate_col = xr_col + 2 * lru_width
    h = x.reshape(batch * seq, d)
    ones = jnp.ones((d,), F32)
    for l in range(depth):
        last = l == depth - 1
        mod = _modulation(c, w_ada[l], b_ada[l])
        mod4 = mod.reshape(batch, N_MOD // 3, 3, d)

        h, (w_in_bf,) = _ffn(h, mod4, 0, norm_ffn1[l], w_ffn1_in[l].astype(BF16),
                             w_ffn1_out[l].astype(BF16), ones, seq, False,
                             to_cast=((w_in, l),))

        proj = _inproj(h, mod4, norm_mix[l], w_in_bf, seq)
        y_attn, (w_ba, w_bl, w_o, w_f2_in, w_f2_out) = _attention(
            proj, batch, seq,
            to_cast=((w_branch_attn, l), (w_branch_lru, l), (w_out, l), (w_ffn2_in, l),
                     (w_ffn2_out, l)))
        w_gates = jnp.concatenate([w_rg_gate[l], w_in_gate[l]], axis=-1).astype(BF16)
        y_lru = _lru(proj, xr_col, conv_w[l], conv_b[l], w_gates, b_rg_gate[l],
                     b_in_gate[l], lru_lambda[l], batch, seq)
        h = _merge(h, y_attn, y_lru, proj, gate_col, mod4, w_ba, w_bl, w_o, seq)

        h, _ = _ffn(h, mod4, 2, norm_ffn2[l], w_f2_in, w_f2_out,
                    norm_final if last else ones, seq, last)
    return h.reshape(batch, seq, d)
```

```python
import functools

import jax
import jax.numpy as jnp
from jax import lax
from jax.experimental import pallas as pl
from jax.experimental.pallas import tpu as pltpu

F32 = jnp.float32
BF16 = jnp.bfloat16

EPS = 1e-6
N_MOD = 9
N_ATTN_HEADS = 8
ATTN_HEAD_DIM = 128
LRU_BLOCKS = 8
CONV_WIDTH = 4
LRU_C = 8.0
LOG2E = 1.4426950408889634

V7X_LANES = 128
V7X_SUBLANES = 8
V7X_VMEM_LIMIT_BYTES = 56 * 1024 * 1024

MOD_TN = 1024
FFN_TM = 512
FFN_TF = 512
FFN_CHAINS = 2
INPROJ_TM = 1024
INPROJ_TN = 1024
ATTN_T = 256
ATTN_PAIRS = 4
ATTN_SKEW = (2, 3, 4)
LRU_TS = 512
NORM_UNROLL = 8
NORM_ROWS = 16


def _params(*sem):
    return pltpu.CompilerParams(dimension_semantics=sem,
                                vmem_limit_bytes=V7X_VMEM_LIMIT_BYTES)


def _softplus(x):
    return jnp.maximum(x, 0.0) + jnp.log(1.0 + jnp.exp(-jnp.abs(x)))


def _norm_modulate_rows(x_ref, gain_ref, mod_ref, y_ref):
    gain_scale = gain_ref[...] * (1.0 + mod_ref[1:2, :])
    shift = mod_ref[0:1, :]

    def body(r, carry):
        rows = pl.ds(pl.multiple_of(r * NORM_ROWS, NORM_ROWS), NORM_ROWS)
        x = x_ref[rows, :]
        y = x * lax.rsqrt(jnp.mean(x * x, axis=-1, keepdims=True) + EPS)
        y_ref[rows, :] = (y * gain_scale + shift).astype(y_ref.dtype)
        return carry

    lax.fori_loop(0, x_ref.shape[0] // NORM_ROWS, body, 0, unroll=NORM_UNROLL)


def _cast_block(shape, nsteps):
    r, c = shape
    row_tile = 2 * V7X_SUBLANES
    best = None
    for row_blocks in range(1, nsteps + 1):
        if nsteps % row_blocks or r % row_blocks or (r // row_blocks) % row_tile:
            continue
        col_blocks = nsteps // row_blocks
        if c % col_blocks or (c // col_blocks) % V7X_LANES:
            continue
        if best is None or c // col_blocks > best[1]:
            best = (r // row_blocks, c // col_blocks)
    assert best is not None, (shape, nsteps)
    return best


def _cast_specs(to_cast, nsteps, step_of):
    in_specs, out_specs, out_shapes = [], [], []
    for w, layer in to_cast:
        rows, cols = _cast_block(w.shape[1:], nsteps)
        col_blocks = w.shape[2] // cols

        def in_map(*idx, layer=layer, col_blocks=col_blocks):
            s = step_of(*idx)
            return (layer, s // col_blocks, s % col_blocks)

        def out_map(*idx, col_blocks=col_blocks):
            s = step_of(*idx)
            return (s // col_blocks, s % col_blocks)

        in_specs.append(pl.BlockSpec((None, rows, cols), in_map))
        out_specs.append(pl.BlockSpec((rows, cols), out_map))
        out_shapes.append(jax.ShapeDtypeStruct(w.shape[1:], BF16))
    return in_specs, out_specs, out_shapes


def _mod_kernel(c_ref, w_ref, b_ref, o_ref):
    c = c_ref[...]
    c_act = (c * jax.nn.sigmoid(c)).astype(BF16)
    o_ref[...] = jnp.dot(c_act, w_ref[...].astype(BF16),
                         preferred_element_type=F32) + b_ref[...]


def _modulation(c, w_ada, b_ada):
    batch, d = c.shape
    n = w_ada.shape[1]
    return pl.pallas_call(
        _mod_kernel,
        out_shape=jax.ShapeDtypeStruct((batch, n), F32),
        grid=(n // MOD_TN,),
        in_specs=[pl.BlockSpec((batch, d), lambda j: (0, 0)),
                  pl.BlockSpec((d, MOD_TN), lambda j: (0, j)),
                  pl.BlockSpec((1, MOD_TN), lambda j: (0, j))],
        out_specs=pl.BlockSpec((batch, MOD_TN), lambda j: (0, j)),
        compiler_params=_params("arbitrary"),
        name="adaln_mod",
    )(c, w_ada, b_ada.reshape(1, n))


def _ffn_kernel(*refs, final_norm, n_cast):
    x_ref, mod_ref, gain_ref, wg_ref, wu_ref, wo_ref, fgain_ref = refs[:7]
    cast_in = refs[7:7 + n_cast]
    o_ref = refs[7 + n_cast]
    cast_out = refs[8 + n_cast:8 + 2 * n_cast]
    y_scr = refs[-1]
    j = pl.program_id(1)

    @pl.when(j == 0)
    def _():
        _norm_modulate_rows(x_ref, gain_ref, mod_ref, y_scr)
        o_ref[...] = jnp.zeros_like(o_ref)
        for src, dst in zip(cast_in, cast_out):
            dst[...] = src[...].astype(dst.dtype)

    y = y_scr[...]
    width = wg_ref.shape[1] // FFN_CHAINS
    part = None
    for n in range(FFN_CHAINS):
        sl = slice(n * width, (n + 1) * width)
        g = jnp.dot(y, wg_ref[:, sl], preferred_element_type=F32)
        u = jnp.dot(y, wu_ref[:, sl], preferred_element_type=F32)
        a = (g * jax.nn.sigmoid(g) * u).astype(BF16)
        down = jnp.dot(a, wo_ref[sl, :], preferred_element_type=F32)
        part = down if part is None else part + down
    o_ref[...] += part

    @pl.when(j == pl.num_programs(1) - 1)
    def _():
        half_gate = 0.5 * mod_ref[2:3, :]
        if final_norm:
            for r in range(o_ref.shape[0] // NORM_ROWS):
                rows = slice(r * NORM_ROWS, (r + 1) * NORM_ROWS)
                h = x_ref[rows, :] + half_gate * o_ref[rows, :]
                h = h * lax.rsqrt(jnp.mean(h * h, axis=-1, keepdims=True) + EPS)
                o_ref[rows, :] = h * fgain_ref[...]
        else:
            o_ref[...] = x_ref[...] + half_gate * o_ref[...]


def _ffn(x2d, mod4, which, gain, w_in, w_out, fgain, seq, final_norm, to_cast=()):
    m, d = x2d.shape
    f = w_out.shape[0]
    nf = f // FFN_TF
    ni = m // FFN_TM
    blocks_per_seq = seq // FFN_TM
    cast_in, cast_out, cast_shapes = _cast_specs(to_cast, ni, lambda i, j: i)
    outs = pl.pallas_call(
        functools.partial(_ffn_kernel, final_norm=final_norm, n_cast=len(to_cast)),
        out_shape=[jax.ShapeDtypeStruct((m, d), F32)] + cast_shapes,
        grid=(ni, nf),
        in_specs=[
            pl.BlockSpec((FFN_TM, d), lambda i, j: (i, 0)),
            pl.BlockSpec((None, None, 3, d),
                         lambda i, j: (i // blocks_per_seq, which, 0, 0)),
            pl.BlockSpec((1, d), lambda i, j: (0, 0)),
            pl.BlockSpec((d, FFN_TF), lambda i, j: (0, j)),
            pl.BlockSpec((d, FFN_TF), lambda i, j: (0, j + nf)),
            pl.BlockSpec((FFN_TF, d), lambda i, j: (j, 0)),
            pl.BlockSpec((1, d), lambda i, j: (0, 0)),
        ] + cast_in,
        out_specs=[pl.BlockSpec((FFN_TM, d), lambda i, j: (i, 0))] + cast_out,
        scratch_shapes=[pltpu.VMEM((FFN_TM, d), BF16)],
        compiler_params=_params("arbitrary", "arbitrary"),
        name="ffn_final" if final_norm else "ffn",
    )(x2d, mod4, gain.reshape(1, d), w_in, w_in, w_out, fgain.reshape(1, d),
      *[w for w, _ in to_cast])
    return outs[0], outs[1:]


def _inproj_kernel(h_ref, mod_ref, gain_ref, w_ref, o_ref, y_scr):
    @pl.when(pl.program_id(1) == 0)
    def _():
        _norm_modulate_rows(h_ref, gain_ref, mod_ref, y_scr)

    o_ref[...] = jnp.dot(y_scr[...], w_ref[...],
                         preferred_element_type=F32).astype(o_ref.dtype)


def _inproj(h2d, mod4, gain, w_in, seq):
    m, d = h2d.shape
    n = w_in.shape[1]
    blocks_per_seq = seq // INPROJ_TM
    return pl.pallas_call(
        _inproj_kernel,
        out_shape=jax.ShapeDtypeStruct((m, n), BF16),
        grid=(m // INPROJ_TM, n // INPROJ_TN),
        in_specs=[
            pl.BlockSpec((INPROJ_TM, d), lambda i, j: (i, 0)),
            pl.BlockSpec((None, None, 3, d),
                         lambda i, j: (i // blocks_per_seq, 1, 0, 0)),
            pl.BlockSpec((1, d), lambda i, j: (0, 0)),
            pl.BlockSpec((d, INPROJ_TN), lambda i, j: (0, j)),
        ],
        out_specs=pl.BlockSpec((INPROJ_TM, INPROJ_TN), lambda i, j: (i, j)),
        scratch_shapes=[pltpu.VMEM((INPROJ_TM, d), BF16)],
        compiler_params=_params("arbitrary", "arbitrary"),
        name="mixer_inproj",
    )(h2d, mod4, gain.reshape(1, d), w_in)


def _attn_kernel(*refs, n_cast):
    n_side = 2 * ATTN_PAIRS
    q_refs = refs[:n_side]
    k_ref, v_ref, tri_ref = refs[n_side:n_side + 3]
    n_in = n_side + 3
    o_ref = refs[n_in + n_cast]
    q_scr, acc_scr, car_scr = refs[-3:]
    for src, dst in zip(refs[n_in:n_in + n_cast], refs[n_in + n_cast + 1:n_in + 2 * n_cast + 1]):
        dst[...] = src[...].astype(dst.dtype)

    t = ATTN_T
    nq = k_ref.shape[0] // t
    yscale = ATTN_HEAD_DIM ** -0.5 * LOG2E
    row = lax.broadcasted_iota(jnp.int32, (t, t), 0)
    col = lax.broadcasted_iota(jnp.int32, (t, t), 1)
    before = col < row

    for side in range(n_side):
        q_scr[side] = (q_refs[side][...].astype(F32) * yscale).astype(BF16)

    pairs = [pl.program_id(2) * ATTN_PAIRS + n for n in range(ATTN_PAIRS)]
    blocks = []
    for n, p in enumerate(pairs):
        blocks += [(2 * n, p, True), (2 * n + 1, nq - 1 - p, True)]
    for s in range(nq - 1):
        for n, p in enumerate(pairs):
            blocks.append((2 * n + jnp.where(s >= p, 1, 0),
                           jnp.where(s >= p, nq - 2 - s, p - 1 - s), False))

    def scores(blk):
        side, kblk, _ = blk
        k0 = pl.multiple_of(kblk * t, t)
        return lax.dot_general(q_scr[side], k_ref[pl.ds(k0, t), :],
                               (((1,), (1,)), ((), ())), preferred_element_type=F32)

    def log_keep(blk, y):
        pos = jnp.maximum(y, 0.0) + jnp.log(1.0 + jnp.exp2(-jnp.abs(y))) * LOG2E
        if blk[2]:
            pos = jnp.where(before, pos, 0.0)
        return y, jnp.dot(pos.astype(BF16), tri_ref[...], preferred_element_type=F32)

    def weights(blk, y, csum):
        side, kblk, diag = blk
        total = jnp.broadcast_to(csum[:, 0:1], (t, V7X_LANES))
        if diag:
            car_scr[side] = total
            w = jnp.where(before, jnp.exp2(y - csum), 0.0)
        else:
            carry = car_scr[side]
            car_scr[side] = carry + total
            w = jnp.exp2(y - csum - jnp.concatenate([carry, carry], axis=1))
        k0 = pl.multiple_of(kblk * t, t)
        return jnp.dot(w.astype(BF16), v_ref[pl.ds(k0, t), :], preferred_element_type=F32)

    def accumulate(blk, pv):
        if blk[2]:
            acc_scr[blk[0]] = pv
        else:
            acc_scr[blk[0]] += pv

    nb = len(blocks)
    raw, ycs, pvs = {}, {}, {}
    d1, d2, d3 = ATTN_SKEW
    for step in range(nb + d3):
        if step < nb:
            raw[step] = scores(blocks[step])
        if 0 <= step - d1 < nb:
            ycs[step - d1] = log_keep(blocks[step - d1], raw.pop(step - d1))
        if 0 <= step - d2 < nb:
            pvs[step - d2] = weights(blocks[step - d2], *ycs.pop(step - d2))
        if 0 <= step - d3 < nb:
            accumulate(blocks[step - d3], pvs.pop(step - d3))

    for side in range(n_side):
        o_ref[side // 2, side % 2] = acc_scr[side].astype(o_ref.dtype)


def _attention(proj, batch, seq, to_cast=()):
    t = ATTN_T
    nq = seq // t
    steps = nq // 2 // ATTN_PAIRS
    n_side = 2 * ATTN_PAIRS
    width = N_ATTN_HEADS * ATTN_HEAD_DIM
    idx = jnp.arange(t)
    upper = (idx[:, None] >= idx[None, :]).astype(BF16)
    cast_in, cast_out, cast_shapes = _cast_specs(
        to_cast, batch * N_ATTN_HEADS * steps,
        lambda b, h, g: (b * N_ATTN_HEADS + h) * steps + g)

    def q_spec(side):
        n, second = side // 2, side % 2

        def index(b, h, g):
            p = g * ATTN_PAIRS + n
            return (b * nq + (nq - 1 - p if second else p), h)

        return pl.BlockSpec((t, ATTN_HEAD_DIM), index)

    outs = pl.pallas_call(
        functools.partial(_attn_kernel, n_cast=len(to_cast)),
        out_shape=[jax.ShapeDtypeStruct((batch, nq // 2, 2, t, width), BF16)] + cast_shapes,
        grid=(batch, N_ATTN_HEADS, steps),
        in_specs=[q_spec(side) for side in range(n_side)] + [
            pl.BlockSpec((seq, ATTN_HEAD_DIM), lambda b, h, g: (b, N_ATTN_HEADS + h)),
            pl.BlockSpec((seq, ATTN_HEAD_DIM), lambda b, h, g: (b, 2 * N_ATTN_HEADS + h)),
            pl.BlockSpec((t, t), lambda b, h, g: (0, 0)),
        ] + cast_in,
        out_specs=[pl.BlockSpec((None, ATTN_PAIRS, 2, t, ATTN_HEAD_DIM),
                                lambda b, h, g: (b, g, 0, 0, h))] + cast_out,
        scratch_shapes=[pltpu.VMEM((n_side, t, ATTN_HEAD_DIM), BF16),
                        pltpu.VMEM((n_side, t, ATTN_HEAD_DIM), F32),
                        pltpu.VMEM((n_side, t, V7X_LANES), F32)],
        compiler_params=_params("arbitrary", "arbitrary", "arbitrary"),
        name="stickbreak_attn",
    )(*([proj] * (n_side + 2)), upper, *[w for w, _ in to_cast])
    return outs[0], outs[1:]


def _lru_kernel(xr_ref, gr_ref, cw_ref, cb_ref, wg_ref, br_ref, bi_ref, lam_ref,
                o_ref, xbuf, hcar, a_scr, u_scr):
    ts = LRU_TS
    c = xr_ref.shape[1]
    blk = c // LRU_BLOCKS
    step = pl.program_id(1)
    halo = V7X_SUBLANES

    @pl.when(step == 0)
    def _():
        xbuf[0:halo, :] = jnp.zeros((halo, c), F32)
        hcar[...] = jnp.zeros_like(hcar)

    @pl.when(step > 0)
    def _():
        xbuf[0:halo, :] = xbuf[ts:ts + halo, :]

    xbuf[halo:ts + halo, :] = xr_ref[...].astype(F32)

    xc = cb_ref[...] + cw_ref[0:1, :] * xbuf[pl.ds(halo - (CONV_WIDTH - 1), ts), :]
    for k in range(1, CONV_WIDTH):
        xc = xc + cw_ref[k:k + 1, :] * xbuf[pl.ds(halo - (CONV_WIDTH - 1) + k, ts), :]

    sp = _softplus(-lam_ref[...])
    for n in range(LRU_BLOCKS):
        sl = slice(n * blk, (n + 1) * blk)
        xn = xc[:, sl]
        gates = jnp.dot(xn.astype(BF16), wg_ref[n], preferred_element_type=F32)
        r = jax.nn.sigmoid(gates[:, :blk] + br_ref[:, sl])
        ig = jax.nn.sigmoid(gates[:, blk:] + bi_ref[:, sl])
        log_a = (-LRU_C * r) * sp[:, sl]
        a = jnp.exp(log_a)
        a_scr[:, sl] = a
        u_scr[:, sl] = jnp.sqrt(-jnp.tanh(log_a) * (1.0 + a * a)) * (ig * xn)

    row = lax.broadcasted_iota(jnp.int32, (V7X_SUBLANES, c), 0)

    def body(g, h):
        r0 = pl.multiple_of(g * V7X_SUBLANES, V7X_SUBLANES)
        a = a_scr[pl.ds(r0, V7X_SUBLANES), :]
        u = u_scr[pl.ds(r0, V7X_SUBLANES), :]
        for sh in (1, 2, 4):
            a_prev = jnp.where(row >= sh, pltpu.roll(a, sh, 0), 1.0)
            u_prev = jnp.where(row >= sh, pltpu.roll(u, sh, 0), 0.0)
            u = a * u_prev + u
            a = a * a_prev
        hh = a * h + u
        u_scr[pl.ds(r0, V7X_SUBLANES), :] = hh
        return hh[V7X_SUBLANES - 1:V7X_SUBLANES, :]

    hcar[...] = lax.fori_loop(0, ts // V7X_SUBLANES, body, hcar[...])
    o_ref[...] = (u_scr[...] * jax.nn.gelu(gr_ref[...].astype(F32))).astype(o_ref.dtype)


def _lru(proj, xr_col, conv_w, conv_b, w_gates, b_r, b_i, lam, batch, seq):
    m = proj.shape[0]
    c = conv_w.shape[1]
    nt = seq // LRU_TS
    xr_blk = xr_col // c
    return pl.pallas_call(
        _lru_kernel,
        out_shape=jax.ShapeDtypeStruct((m, c), BF16),
        grid=(batch, nt),
        in_specs=[
            pl.BlockSpec((LRU_TS, c), lambda b, s: (b * nt + s, xr_blk)),
            pl.BlockSpec((LRU_TS, c), lambda b, s: (b * nt + s, xr_blk + 1)),
            pl.BlockSpec((CONV_WIDTH, c), lambda b, s: (0, 0)),
            pl.BlockSpec((1, c), lambda b, s: (0, 0)),
            pl.BlockSpec(w_gates.shape, lambda b, s: (0, 0, 0)),
            pl.BlockSpec((1, c), lambda b, s: (0, 0)),
            pl.BlockSpec((1, c), lambda b, s: (0, 0)),
            pl.BlockSpec((1, c), lambda b, s: (0, 0)),
        ],
        out_specs=pl.BlockSpec((LRU_TS, c), lambda b, s: (b * nt + s, 0)),
        scratch_shapes=[pltpu.VMEM((LRU_TS + V7X_SUBLANES, c), F32),
                        pltpu.VMEM((1, c), F32),
                        pltpu.VMEM((LRU_TS, c), F32),
                        pltpu.VMEM((LRU_TS, c), F32)],
        compiler_params=_params("arbitrary", "arbitrary"),
        name="rg_lru",
    )(proj, proj, conv_w, conv_b.reshape(1, c), w_gates, b_r.reshape(1, c),
      b_i.reshape(1, c), lam.reshape(1, c))


def _merge_kernel(h_ref, ya_ref, yl_ref, ga0_ref, ga1_ref, gl0_ref, gl1_ref, mod_ref,
                  wa_ref, wl_ref, wo_ref, o_ref):
    pa = jnp.dot(ya_ref[...], wa_ref[...], preferred_element_type=F32)
    plru = jnp.dot(yl_ref[...], wl_ref[...], preferred_element_type=F32)
    ga = jnp.concatenate([ga0_ref[...], ga1_ref[...]], axis=1).astype(F32)
    gl = jnp.concatenate([gl0_ref[...], gl1_ref[...]], axis=1).astype(F32)
    merged = jax.nn.sigmoid(ga) * pa + jax.nn.sigmoid(gl) * plru
    out = jnp.dot(merged.astype(BF16), wo_ref[...], preferred_element_type=F32)
    o_ref[...] = h_ref[...] + mod_ref[2:3, :] * out


def _merge(h2d, y_attn, y_lru, proj, gate_col, mod4, w_ba, w_bl, w_o, seq):
    m, d = h2d.shape
    tm = ATTN_T
    nq = seq // tm
    wa = y_attn.shape[-1]
    wl = y_lru.shape[1]
    half = d // 2
    g0 = gate_col // half
    resident = pl.Buffered(1)

    def attn_block(i):
        ii = i % nq
        first = ii < nq // 2
        return (i // nq, jnp.where(first, ii, nq - 1 - ii), jnp.where(first, 0, 1), 0, 0)

    return pl.pallas_call(
        _merge_kernel,
        out_shape=jax.ShapeDtypeStruct((m, d), F32),
        grid=(m // tm,),
        in_specs=[
            pl.BlockSpec((tm, d), lambda i: (i, 0)),
            pl.BlockSpec((None, None, None, tm, wa), attn_block),
            pl.BlockSpec((tm, wl), lambda i: (i, 0)),
            pl.BlockSpec((tm, half), lambda i: (i, g0)),
            pl.BlockSpec((tm, half), lambda i: (i, g0 + 1)),
            pl.BlockSpec((tm, half), lambda i: (i, g0 + 2)),
            pl.BlockSpec((tm, half), lambda i: (i, g0 + 3)),
            pl.BlockSpec((None, None, 3, d), lambda i: (i // nq, 1, 0, 0)),
            pl.BlockSpec((wa, d), lambda i: (0, 0), pipeline_mode=resident),
            pl.BlockSpec((wl, d), lambda i: (0, 0), pipeline_mode=resident),
            pl.BlockSpec((d, d), lambda i: (0, 0), pipeline_mode=resident),
        ],
        out_specs=pl.BlockSpec((tm, d), lambda i: (i, 0)),
        compiler_params=_params("arbitrary"),
        name="merge_outproj",
    )(h2d, y_attn, y_lru, proj, proj, proj, proj, mod4, w_ba, w_bl, w_o)


def kernel(x, c, w_ada, b_ada, norm_ffn1, w_ffn1_in, w_ffn1_out, norm_mix, w_in,
           conv_w, conv_b, w_rg_gate, b_rg_gate, w_in_gate, b_in_gate, lru_lambda,
           w_branch_attn, w_branch_lru, w_out, norm_ffn2, w_ffn2_in, w_ffn2_out,
           norm_final):
    batch, seq, d = x.shape
    depth = w_ada.shape[0]
    assert depth >= 1, "the final RMSNorm is fused into the last layer"
    attn_width = N_ATTN_HEADS * ATTN_HEAD_DIM
    lru_width = conv_w.shape[-1]
    xr_col = 3 * attn_width
    gate_col = xr_col + 2 * lru_width
    h = x.reshape(batch * seq, d)
    ones = jnp.ones((d,), F32)
    for l in range(depth):
        last = l == depth - 1
        mod = _modulation(c, w_ada[l], b_ada[l])
        mod4 = mod.reshape(batch, N_MOD // 3, 3, d)

        h, (w_in_bf,) = _ffn(h, mod4, 0, norm_ffn1[l], w_ffn1_in[l].astype(BF16),
                             w_ffn1_out[l].astype(BF16), ones, seq, False,
                             to_cast=((w_in, l),))

        proj = _inproj(h, mod4, norm_mix[l], w_in_bf, seq)
        y_attn, (w_ba, w_bl, w_o, w_f2_in, w_f2_out) = _attention(
            proj, batch, seq,
            to_cast=((w_branch_attn, l), (w_branch_lru, l), (w_out, l), (w_ffn2_in, l),
                     (w_ffn2_out, l)))
        w_gates = jnp.concatenate([w_rg_gate[l], w_in_gate[l]], axis=-1).astype(BF16)
        y_lru = _lru(proj, xr_col, conv_w[l], conv_b[l], w_gates, b_rg_gate[l],
                     b_in_gate[l], lru_lambda[l], batch, seq)
        h = _merge(h, y_attn, y_lru, proj, gate_col, mod4, w_ba, w_bl, w_o, seq)

        h, _ = _ffn(h, mod4, 2, norm_ffn2[l], w_f2_in, w_f2_out,
                    norm_final if last else ones, seq, last)
    return h.reshape(batch, seq, d)
```

```python
import functools

import jax
import jax.numpy as jnp
from jax import lax
from jax.experimental import pallas as pl
from jax.experimental.pallas import tpu as pltpu

F32 = jnp.float32
BF16 = jnp.bfloat16

EPS = 1e-6
N_MOD = 9
N_ATTN_HEADS = 8
ATTN_HEAD_DIM = 128
LRU_BLOCKS = 8
CONV_WIDTH = 4
LRU_C = 8.0
LOG2E = 1.4426950408889634

V7X_LANES = 128
V7X_SUBLANES = 8
V7X_VMEM_LIMIT_BYTES = 56 * 1024 * 1024

MOD_TN = 1024
FFN_TM = 1024
FFN_TF = 512
FFN_CAST_STEPS = 8
FFN_CHAINS = 2
INPROJ_TM = 1024
INPROJ_TN = 1024
ATTN_T = 256
ATTN_PAIRS = 4
ATTN_SKEW = (2, 3, 4)
LRU_TS = 512
NORM_UNROLL = 8
NORM_ROWS = 16


def _params(*sem):
    return pltpu.CompilerParams(dimension_semantics=sem,
                                vmem_limit_bytes=V7X_VMEM_LIMIT_BYTES)


def _softplus(x):
    return jnp.maximum(x, 0.0) + jnp.log(1.0 + jnp.exp(-jnp.abs(x)))


def _norm_modulate_rows(x_ref, gain_ref, mod_ref, y_ref):
    gain_scale = gain_ref[...] * (1.0 + mod_ref[1:2, :])
    shift = mod_ref[0:1, :]

    def body(r, carry):
        rows = pl.ds(pl.multiple_of(r * NORM_ROWS, NORM_ROWS), NORM_ROWS)
        x = x_ref[rows, :]
        y = x * lax.rsqrt(jnp.mean(x * x, axis=-1, keepdims=True) + EPS)
        y_ref[rows, :] = (y * gain_scale + shift).astype(y_ref.dtype)
        return carry

    lax.fori_loop(0, x_ref.shape[0] // NORM_ROWS, body, 0, unroll=NORM_UNROLL)


def _cast_block(shape, nsteps):
    r, c = shape
    row_tile = 2 * V7X_SUBLANES
    best = None
    for row_blocks in range(1, nsteps + 1):
        if nsteps % row_blocks or r % row_blocks or (r // row_blocks) % row_tile:
            continue
        col_blocks = nsteps // row_blocks
        if c % col_blocks or (c // col_blocks) % V7X_LANES:
            continue
        if best is None or c // col_blocks > best[1]:
            best = (r // row_blocks, c // col_blocks)
    assert best is not None, (shape, nsteps)
    return best


def _cast_specs(to_cast, nsteps, step_of):
    in_specs, out_specs, out_shapes = [], [], []
    for w, layer in to_cast:
        rows, cols = _cast_block(w.shape[1:], nsteps)
        col_blocks = w.shape[2] // cols

        def in_map(*idx, layer=layer, col_blocks=col_blocks):
            s = step_of(*idx)
            return (layer, s // col_blocks, s % col_blocks)

        def out_map(*idx, col_blocks=col_blocks):
            s = step_of(*idx)
            return (s // col_blocks, s % col_blocks)

        in_specs.append(pl.BlockSpec((None, rows, cols), in_map))
        out_specs.append(pl.BlockSpec((rows, cols), out_map))
        out_shapes.append(jax.ShapeDtypeStruct(w.shape[1:], BF16))
    return in_specs, out_specs, out_shapes


def _mod_kernel(c_ref, w_ref, b_ref, o_ref):
    c = c_ref[...]
    c_act = (c * jax.nn.sigmoid(c)).astype(BF16)
    o_ref[...] = jnp.dot(c_act, w_ref[...].astype(BF16),
                         preferred_element_type=F32) + b_ref[...]


def _modulation(c, w_ada, b_ada):
    batch, d = c.shape
    n = w_ada.shape[1]
    return pl.pallas_call(
        _mod_kernel,
        out_shape=jax.ShapeDtypeStruct((batch, n), F32),
        grid=(n // MOD_TN,),
        in_specs=[pl.BlockSpec((batch, d), lambda j: (0, 0)),
                  pl.BlockSpec((d, MOD_TN), lambda j: (0, j)),
                  pl.BlockSpec((1, MOD_TN), lambda j: (0, j))],
        out_specs=pl.BlockSpec((batch, MOD_TN), lambda j: (0, j)),
        compiler_params=_params("arbitrary"),
        name="adaln_mod",
    )(c, w_ada, b_ada.reshape(1, n))


def _ffn_kernel(*refs, final_norm, n_cast):
    x_hbm, mod_ref, gain_ref, wg_ref, wu_ref, wo_ref, fgain_ref = refs[:7]
    cast_in = refs[7:7 + n_cast]
    o_ref = refs[7 + n_cast]
    cast_out = refs[8 + n_cast:8 + 2 * n_cast]
    y_scr, x_scr, x_sem = refs[-3:]
    i = pl.program_id(0)
    j = pl.program_id(1)
    tm = x_scr.shape[0]

    def x_copy(blk):
        rows = pl.ds(pl.multiple_of(blk * tm, tm), tm)
        return pltpu.make_async_copy(x_hbm.at[rows, :], x_scr, x_sem)

    @pl.when((i == 0) & (j == 0))
    def _():
        x_copy(0).start()

    @pl.when(j == 0)
    def _():
        x_copy(i).wait()
        _norm_modulate_rows(x_scr, gain_ref, mod_ref, y_scr)
        o_ref[...] = x_scr[...]

    @pl.when((j == 1) & (i + 1 < pl.num_programs(0)))
    def _():
        x_copy(i + 1).start()

    for src, dst in zip(cast_in, cast_out):
        dst[...] = src[...].astype(dst.dtype)

    y = y_scr[...]
    width = wg_ref.shape[1] // FFN_CHAINS
    part = None
    for n in range(FFN_CHAINS):
        sl = slice(n * width, (n + 1) * width)
        g = jnp.dot(y, wg_ref[:, sl], preferred_element_type=F32)
        u = jnp.dot(y, wu_ref[:, sl], preferred_element_type=F32)
        a = (g * jax.nn.sigmoid(g) * u).astype(BF16)
        down = jnp.dot(a, wo_ref[sl, :], preferred_element_type=F32)
        part = down if part is None else part + down
    o_ref[...] += (0.5 * mod_ref[2:3, :]) * part

    if final_norm:
        @pl.when(j == pl.num_programs(1) - 1)
        def _():
            for r in range(o_ref.shape[0] // NORM_ROWS):
                rows = slice(r * NORM_ROWS, (r + 1) * NORM_ROWS)
                h = o_ref[rows, :]
                h = h * lax.rsqrt(jnp.mean(h * h, axis=-1, keepdims=True) + EPS)
                o_ref[rows, :] = h * fgain_ref[...]


def _ffn(x2d, mod4, which, gain, w_in, w_out, fgain, seq, final_norm, to_cast=()):
    m, d = x2d.shape
    f = w_out.shape[0]
    nf = f // FFN_TF
    ni = m // FFN_TM
    blocks_per_seq = seq // FFN_TM
    assert nf >= 2, "the x refill is started in the second hidden step"
    cast_steps = min(FFN_CAST_STEPS, nf)
    cast_in, cast_out, cast_shapes = _cast_specs(
        to_cast, ni * cast_steps, lambda i, j: i * cast_steps + jnp.minimum(j, cast_steps - 1))
    outs = pl.pallas_call(
        functools.partial(_ffn_kernel, final_norm=final_norm, n_cast=len(to_cast)),
        out_shape=[jax.ShapeDtypeStruct((m, d), F32)] + cast_shapes,
        grid=(ni, nf),
        in_specs=[
            pl.BlockSpec(memory_space=pl.ANY),
            pl.BlockSpec((None, None, 3, d),
                         lambda i, j: (i // blocks_per_seq, which, 0, 0)),
            pl.BlockSpec((1, d), lambda i, j: (0, 0)),
            pl.BlockSpec((d, FFN_TF), lambda i, j: (0, j)),
            pl.BlockSpec((d, FFN_TF), lambda i, j: (0, j + nf)),
            pl.BlockSpec((FFN_TF, d), lambda i, j: (j, 0)),
            pl.BlockSpec((1, d), lambda i, j: (0, 0)),
        ] + cast_in,
        out_specs=[pl.BlockSpec((FFN_TM, d), lambda i, j: (i, 0))] + cast_out,
        scratch_shapes=[pltpu.VMEM((FFN_TM, d), BF16),
                        pltpu.VMEM((FFN_TM, d), F32),
                        pltpu.SemaphoreType.DMA(())],
        compiler_params=_params("arbitrary", "arbitrary"),
        name="ffn_final" if final_norm else "ffn",
    )(x2d, mod4, gain.reshape(1, d), w_in, w_in, w_out, fgain.reshape(1, d),
      *[w for w, _ in to_cast])
    return outs[0], outs[1:]


def _inproj_kernel(h_ref, mod_ref, gain_ref, w_ref, o_ref, y_scr):
    @pl.when(pl.program_id(1) == 0)
    def _():
        _norm_modulate_rows(h_ref, gain_ref, mod_ref, y_scr)

    o_ref[...] = jnp.dot(y_scr[...], w_ref[...],
                         preferred_element_type=F32).astype(o_ref.dtype)


def _inproj(h2d, mod4, gain, w_in, seq):
    m, d = h2d.shape
    n = w_in.shape[1]
    blocks_per_seq = seq // INPROJ_TM
    return pl.pallas_call(
        _inproj_kernel,
        out_shape=jax.ShapeDtypeStruct((m, n), BF16),
        grid=(m // INPROJ_TM, n // INPROJ_TN),
        in_specs=[
            pl.BlockSpec((INPROJ_TM, d), lambda i, j: (i, 0)),
            pl.BlockSpec((None, None, 3, d),
                         lambda i, j: (i // blocks_per_seq, 1, 0, 0)),
            pl.BlockSpec((1, d), lambda i, j: (0, 0)),
            pl.BlockSpec((d, INPROJ_TN), lambda i, j: (0, j)),
        ],
        out_specs=pl.BlockSpec((INPROJ_TM, INPROJ_TN), lambda i, j: (i, j)),
        scratch_shapes=[pltpu.VMEM((INPROJ_TM, d), BF16)],
        compiler_params=_params("arbitrary", "arbitrary"),
        name="mixer_inproj",
    )(h2d, mod4, gain.reshape(1, d), w_in)


def _attn_kernel(*refs, n_cast):
    n_side = 2 * ATTN_PAIRS
    q_refs = refs[:n_side]
    k_ref, v_ref, tri_ref = refs[n_side:n_side + 3]
    n_in = n_side + 3
    o_ref = refs[n_in + n_cast]
    q_scr, acc_scr, car_scr = refs[-3:]
    for src, dst in zip(refs[n_in:n_in + n_cast], refs[n_in + n_cast + 1:n_in + 2 * n_cast + 1]):
        dst[...] = src[...].astype(dst.dtype)

    t = ATTN_T
    nq = k_ref.shape[0] // t
    yscale = ATTN_HEAD_DIM ** -0.5 * LOG2E
    row = lax.broadcasted_iota(jnp.int32, (t, t), 0)
    col = lax.broadcasted_iota(jnp.int32, (t, t), 1)
    before = col < row

    for side in range(n_side):
        q_scr[side] = (q_refs[side][...].astype(F32) * yscale).astype(BF16)

    pairs = [pl.program_id(2) * ATTN_PAIRS + n for n in range(ATTN_PAIRS)]
    blocks = []
    for n, p in enumerate(pairs):
        blocks += [(2 * n, p, True), (2 * n + 1, nq - 1 - p, True)]
    for s in range(nq - 1):
        for n, p in enumerate(pairs):
            blocks.append((2 * n + jnp.where(s >= p, 1, 0),
                           jnp.where(s >= p, nq - 2 - s, p - 1 - s), False))

    def scores(blk):
        side, kblk, _ = blk
        k0 = pl.multiple_of(kblk * t, t)
        return lax.dot_general(q_scr[side], k_ref[pl.ds(k0, t), :],
                               (((1,), (1,)), ((), ())), preferred_element_type=F32)

    def log_keep(blk, y):
        pos = jnp.maximum(y, 0.0) + jnp.log(1.0 + jnp.exp2(-jnp.abs(y))) * LOG2E
        if blk[2]:
            pos = jnp.where(before, pos, 0.0)
        return y, jnp.dot(pos.astype(BF16), tri_ref[...], preferred_element_type=F32)

    def weights(blk, y, csum):
        side, kblk, diag = blk
        total = jnp.broadcast_to(csum[:, 0:1], (t, V7X_LANES))
        if diag:
            car_scr[side] = total
            w = jnp.where(before, jnp.exp2(y - csum), 0.0)
        else:
            carry = car_scr[side]
            car_scr[side] = carry + total
            w = jnp.exp2(y - csum - jnp.concatenate([carry, carry], axis=1))
        k0 = pl.multiple_of(kblk * t, t)
        return jnp.dot(w.astype(BF16), v_ref[pl.ds(k0, t), :], preferred_element_type=F32)

    def accumulate(blk, pv):
        if blk[2]:
            acc_scr[blk[0]] = pv
        else:
            acc_scr[blk[0]] += pv

    nb = len(blocks)
    raw, ycs, pvs = {}, {}, {}
    d1, d2, d3 = ATTN_SKEW
    for step in range(nb + d3):
        if step < nb:
            raw[step] = scores(blocks[step])
        if 0 <= step - d1 < nb:
            ycs[step - d1] = log_keep(blocks[step - d1], raw.pop(step - d1))
        if 0 <= step - d2 < nb:
            pvs[step - d2] = weights(blocks[step - d2], *ycs.pop(step - d2))
        if 0 <= step - d3 < nb:
            accumulate(blocks[step - d3], pvs.pop(step - d3))

    for side in range(n_side):
        o_ref[side // 2, side % 2] = acc_scr[side].astype(o_ref.dtype)


def _attention(proj, batch, seq, to_cast=()):
    t = ATTN_T
    nq = seq // t
    steps = nq // 2 // ATTN_PAIRS
    n_side = 2 * ATTN_PAIRS
    width = N_ATTN_HEADS * ATTN_HEAD_DIM
    idx = jnp.arange(t)
    upper = (idx[:, None] >= idx[None, :]).astype(BF16)
    cast_in, cast_out, cast_shapes = _cast_specs(
        to_cast, batch * N_ATTN_HEADS * steps,
        lambda b, h, g: (b * N_ATTN_HEADS + h) * steps + g)

    def q_spec(side):
        n, second = side // 2, side % 2

        def index(b, h, g):
            p = g * ATTN_PAIRS + n
            return (b * nq + (nq - 1 - p if second else p), h)

        return pl.BlockSpec((t, ATTN_HEAD_DIM), index)

    outs = pl.pallas_call(
        functools.partial(_attn_kernel, n_cast=len(to_cast)),
        out_shape=[jax.ShapeDtypeStruct((batch, nq // 2, 2, t, width), BF16)] + cast_shapes,
        grid=(batch, N_ATTN_HEADS, steps),
        in_specs=[q_spec(side) for side in range(n_side)] + [
            pl.BlockSpec((seq, ATTN_HEAD_DIM), lambda b, h, g: (b, N_ATTN_HEADS + h)),
            pl.BlockSpec((seq, ATTN_HEAD_DIM), lambda b, h, g: (b, 2 * N_ATTN_HEADS + h)),
            pl.BlockSpec((t, t), lambda b, h, g: (0, 0)),
        ] + cast_in,
        out_specs=[pl.BlockSpec((None, ATTN_PAIRS, 2, t, ATTN_HEAD_DIM),
                                lambda b, h, g: (b, g, 0, 0, h))] + cast_out,
        scratch_shapes=[pltpu.VMEM((n_side, t, ATTN_HEAD_DIM), BF16),
                        pltpu.VMEM((n_side, t, ATTN_HEAD_DIM), F32),
                        pltpu.VMEM((n_side, t, V7X_LANES), F32)],
        compiler_params=_params("arbitrary", "arbitrary", "arbitrary"),
        name="stickbreak_attn",
    )(*([proj] * (n_side + 2)), upper, *[w for w, _ in to_cast])
    return outs[0], outs[1:]


def _lru_kernel(xr_ref, gr_ref, cw_ref, cb_ref, wg_ref, br_ref, bi_ref, lam_ref,
                o_ref, xbuf, hcar, a_scr, u_scr):
    ts = LRU_TS
    c = xr_ref.shape[1]
    blk = c // LRU_BLOCKS
    step = pl.program_id(1)
    halo = V7X_SUBLANES

    @pl.when(step == 0)
    def _():
        xbuf[0:halo, :] = jnp.zeros((halo, c), F32)
        hcar[...] = jnp.zeros_like(hcar)

    @pl.when(step > 0)
    def _():
        xbuf[0:halo, :] = xbuf[ts:ts + halo, :]

    xbuf[halo:ts + halo, :] = xr_ref[...].astype(F32)

    xc = cb_ref[...] + cw_ref[0:1, :] * xbuf[pl.ds(halo - (CONV_WIDTH - 1), ts), :]
    for k in range(1, CONV_WIDTH):
        xc = xc + cw_ref[k:k + 1, :] * xbuf[pl.ds(halo - (CONV_WIDTH - 1) + k, ts), :]

    sp = _softplus(-lam_ref[...])
    for n in range(LRU_BLOCKS):
        sl = slice(n * blk, (n + 1) * blk)
        xn = xc[:, sl]
        gates = jnp.dot(xn.astype(BF16), wg_ref[n], preferred_element_type=F32)
        r = jax.nn.sigmoid(gates[:, :blk] + br_ref[:, sl])
        ig = jax.nn.sigmoid(gates[:, blk:] + bi_ref[:, sl])
        log_a = (-LRU_C * r) * sp[:, sl]
        a = jnp.exp(log_a)
        a_scr[:, sl] = a
        u_scr[:, sl] = jnp.sqrt(-jnp.tanh(log_a) * (1.0 + a * a)) * (ig * xn)

    row = lax.broadcasted_iota(jnp.int32, (V7X_SUBLANES, c), 0)

    def body(g, h):
        r0 = pl.multiple_of(g * V7X_SUBLANES, V7X_SUBLANES)
        a = a_scr[pl.ds(r0, V7X_SUBLANES), :]
        u = u_scr[pl.ds(r0, V7X_SUBLANES), :]
        for sh in (1, 2, 4):
            a_prev = jnp.where(row >= sh, pltpu.roll(a, sh, 0), 1.0)
            u_prev = jnp.where(row >= sh, pltpu.roll(u, sh, 0), 0.0)
            u = a * u_prev + u
            a = a * a_prev
        hh = a * h + u
        u_scr[pl.ds(r0, V7X_SUBLANES), :] = hh
        return hh[V7X_SUBLANES - 1:V7X_SUBLANES, :]

    hcar[...] = lax.fori_loop(0, ts // V7X_SUBLANES, body, hcar[...])
    o_ref[...] = (u_scr[...] * jax.nn.gelu(gr_ref[...].astype(F32))).astype(o_ref.dtype)


def _lru(proj, xr_col, conv_w, conv_b, w_gates, b_r, b_i, lam, batch, seq):
    m = proj.shape[0]
    c = conv_w.shape[1]
    nt = seq // LRU_TS
    xr_blk = xr_col // c
    return pl.pallas_call(
        _lru_kernel,
        out_shape=jax.ShapeDtypeStruct((m, c), BF16),
        grid=(batch, nt),
        in_specs=[
            pl.BlockSpec((LRU_TS, c), lambda b, s: (b * nt + s, xr_blk)),
            pl.BlockSpec((LRU_TS, c), lambda b, s: (b * nt + s, xr_blk + 1)),
            pl.BlockSpec((CONV_WIDTH, c), lambda b, s: (0, 0)),
            pl.BlockSpec((1, c), lambda b, s: (0, 0)),
            pl.BlockSpec(w_gates.shape, lambda b, s: (0, 0, 0)),
            pl.BlockSpec((1, c), lambda b, s: (0, 0)),
            pl.BlockSpec((1, c), lambda b, s: (0, 0)),
            pl.BlockSpec((1, c), lambda b, s: (0, 0)),
        ],
        out_specs=pl.BlockSpec((LRU_TS, c), lambda b, s: (b * nt + s, 0)),
        scratch_shapes=[pltpu.VMEM((LRU_TS + V7X_SUBLANES, c), F32),
                        pltpu.VMEM((1, c), F32),
                        pltpu.VMEM((LRU_TS, c), F32),
                        pltpu.VMEM((LRU_TS, c), F32)],
        compiler_params=_params("arbitrary", "arbitrary"),
        name="rg_lru",
    )(proj, proj, conv_w, conv_b.reshape(1, c), w_gates, b_r.reshape(1, c),
      b_i.reshape(1, c), lam.reshape(1, c))


def _merge_kernel(h_ref, ya_ref, yl_ref, ga0_ref, ga1_ref, gl0_ref, gl1_ref, mod_ref,
                  wa_ref, wl_ref, wo_ref, o_ref):
    pa = jnp.dot(ya_ref[...], wa_ref[...], preferred_element_type=F32)
    plru = jnp.dot(yl_ref[...], wl_ref[...], preferred_element_type=F32)
    ga = jnp.concatenate([ga0_ref[...], ga1_ref[...]], axis=1).astype(F32)
    gl = jnp.concatenate([gl0_ref[...], gl1_ref[...]], axis=1).astype(F32)
    merged = jax.nn.sigmoid(ga) * pa + jax.nn.sigmoid(gl) * plru
    out = jnp.dot(merged.astype(BF16), wo_ref[...], preferred_element_type=F32)
    o_ref[...] = h_ref[...] + mod_ref[2:3, :] * out


def _merge(h2d, y_attn, y_lru, proj, gate_col, mod4, w_ba, w_bl, w_o, seq):
    m, d = h2d.shape
    tm = ATTN_T
    nq = seq // tm
    wa = y_attn.shape[-1]
    wl = y_lru.shape[1]
    half = d // 2
    g0 = gate_col // half
    resident = pl.Buffered(1)

    def attn_block(i):
        ii = i % nq
        first = ii < nq // 2
        return (i // nq, jnp.where(first, ii, nq - 1 - ii), jnp.where(first, 0, 1), 0, 0)

    return pl.pallas_call(
        _merge_kernel,
        out_shape=jax.ShapeDtypeStruct((m, d), F32),
        grid=(m // tm,),
        in_specs=[
            pl.BlockSpec((tm, d), lambda i: (i, 0)),
            pl.BlockSpec((None, None, None, tm, wa), attn_block),
            pl.BlockSpec((tm, wl), lambda i: (i, 0)),
            pl.BlockSpec((tm, half), lambda i: (i, g0)),
            pl.BlockSpec((tm, half), lambda i: (i, g0 + 1)),
            pl.BlockSpec((tm, half), lambda i: (i, g0 + 2)),
            pl.BlockSpec((tm, half), lambda i: (i, g0 + 3)),
            pl.BlockSpec((None, None, 3, d), lambda i: (i // nq, 1, 0, 0)),
            pl.BlockSpec((wa, d), lambda i: (0, 0), pipeline_mode=resident),
            pl.BlockSpec((wl, d), lambda i: (0, 0), pipeline_mode=resident),
            pl.BlockSpec((d, d), lambda i: (0, 0), pipeline_mode=resident),
        ],
        out_specs=pl.BlockSpec((tm, d), lambda i: (i, 0)),
        compiler_params=_params("arbitrary"),
        name="merge_outproj",
    )(h2d, y_attn, y_lru, proj, proj, proj, proj, mod4, w_ba, w_bl, w_o)


def kernel(x, c, w_ada, b_ada, norm_ffn1, w_ffn1_in, w_ffn1_out, norm_mix, w_in,
           conv_w, conv_b, w_rg_gate, b_rg_gate, w_in_gate, b_in_gate, lru_lambda,
           w_branch_attn, w_branch_lru, w_out, norm_ffn2, w_ffn2_in, w_ffn2_out,
           norm_final):
    batch, seq, d = x.shape
    depth = w_ada.shape[0]
    assert depth >= 1, "the final RMSNorm is fused into the last layer"
    attn_width = N_ATTN_HEADS * ATTN_HEAD_DIM
    lru_width = conv_w.shape[-1]
    xr_col = 3 * attn_width
    gate_col = xr_col + 2 * lru_width
    h = x.reshape(batch * seq, d)
    ones = jnp.ones((d,), F32)
    for l in range(depth):
        last = l == depth - 1
        mod = _modulation(c, w_ada[l], b_ada[l])
        mod4 = mod.reshape(batch, N_MOD // 3, 3, d)

        h, (w_in_bf,) = _ffn(h, mod4, 0, norm_ffn1[l], w_ffn1_in[l].astype(BF16),
                             w_ffn1_out[l].astype(BF16), ones, seq, False,
                             to_cast=((w_in, l),))

        proj = _inproj(h, mod4, norm_mix[l], w_in_bf, seq)
        y_attn, (w_ba, w_bl, w_o, w_f2_in, w_f2_out) = _attention(
            proj, batch, seq,
            to_cast=((w_branch_attn, l), (w_branch_lru, l), (w_out, l), (w_ffn2_in, l),
                     (w_ffn2_out, l)))
        w_gates = jnp.concatenate([w_rg_gate[l], w_in_gate[l]], axis=-1).astype(BF16)
        y_lru = _lru(proj, xr_col, conv_w[l], conv_b[l], w_gates, b_rg_gate[l],
                     b_in_gate[l], lru_lambda[l], batch, seq)
        h = _merge(h, y_attn, y_lru, proj, gate_col, mod4, w_ba, w_bl, w_o, seq)

        h, _ = _ffn(h, mod4, 2, norm_ffn2[l], w_f2_in, w_f2_out,
                    norm_final if last else ones, seq, last)
    return h.reshape(batch, seq, d)
```

```python
import functools

import jax
import jax.numpy as jnp
from jax import lax
from jax.experimental import pallas as pl
from jax.experimental.pallas import tpu as pltpu

F32 = jnp.float32
BF16 = jnp.bfloat16

EPS = 1e-6
N_MOD = 9
N_ATTN_HEADS = 8
ATTN_HEAD_DIM = 128
LRU_BLOCKS = 8
CONV_WIDTH = 4
LRU_C = 8.0
LOG2E = 1.4426950408889634

V7X_LANES = 128
V7X_SUBLANES = 8
V7X_VMEM_LIMIT_BYTES = 56 * 1024 * 1024

MOD_TN = 1024
FFN_TM = 1024
FFN_TF = 512
FFN_CAST_STEPS = 8
FFN_CHAINS = 2
INPROJ_TM = 1024
INPROJ_TN = 1024
ATTN_T = 256
ATTN_PAIRS = 4
ATTN_SKEW = (2, 3, 4)
LRU_TS = 512
NORM_UNROLL = 8
NORM_ROWS = 16


def _params(*sem):
    return pltpu.CompilerParams(dimension_semantics=sem,
                                vmem_limit_bytes=V7X_VMEM_LIMIT_BYTES)


def _softplus(x):
    return jnp.maximum(x, 0.0) + jnp.log(1.0 + jnp.exp(-jnp.abs(x)))


def _norm_modulate_rows(x_ref, gain_ref, mod_ref, y_ref):
    gain_scale = gain_ref[...] * (1.0 + mod_ref[1:2, :])
    shift = mod_ref[0:1, :]

    def body(r, carry):
        rows = pl.ds(pl.multiple_of(r * NORM_ROWS, NORM_ROWS), NORM_ROWS)
        x = x_ref[rows, :]
        y = x * lax.rsqrt(jnp.mean(x * x, axis=-1, keepdims=True) + EPS)
        y_ref[rows, :] = (y * gain_scale + shift).astype(y_ref.dtype)
        return carry

    lax.fori_loop(0, x_ref.shape[0] // NORM_ROWS, body, 0, unroll=NORM_UNROLL)


def _cast_block(shape, nsteps):
    r, c = shape
    row_tile = 2 * V7X_SUBLANES
    best = None
    for row_blocks in range(1, nsteps + 1):
        if nsteps % row_blocks or r % row_blocks or (r // row_blocks) % row_tile:
            continue
        col_blocks = nsteps // row_blocks
        if c % col_blocks or (c // col_blocks) % V7X_LANES:
            continue
        if best is None or c // col_blocks > best[1]:
            best = (r // row_blocks, c // col_blocks)
    assert best is not None, (shape, nsteps)
    return best


def _cast_specs(to_cast, nsteps, step_of):
    in_specs, out_specs, out_shapes = [], [], []
    for w, layer in to_cast:
        rows, cols = _cast_block(w.shape[1:], nsteps)
        col_blocks = w.shape[2] // cols

        def in_map(*idx, layer=layer, col_blocks=col_blocks):
            s = step_of(*idx)
            return (layer, s // col_blocks, s % col_blocks)

        def out_map(*idx, col_blocks=col_blocks):
            s = step_of(*idx)
            return (s // col_blocks, s % col_blocks)

        in_specs.append(pl.BlockSpec((None, rows, cols), in_map))
        out_specs.append(pl.BlockSpec((rows, cols), out_map))
        out_shapes.append(jax.ShapeDtypeStruct(w.shape[1:], BF16))
    return in_specs, out_specs, out_shapes


def _mod_kernel(c_ref, w_ref, b_ref, o_ref):
    c = c_ref[...]
    c_act = (c * jax.nn.sigmoid(c)).astype(BF16)
    o_ref[...] = jnp.dot(c_act, w_ref[...].astype(BF16),
                         preferred_element_type=F32) + b_ref[...]


def _modulation(c, w_ada, b_ada):
    batch, d = c.shape
    n = w_ada.shape[1]
    return pl.pallas_call(
        _mod_kernel,
        out_shape=jax.ShapeDtypeStruct((batch, n), F32),
        grid=(n // MOD_TN,),
        in_specs=[pl.BlockSpec((batch, d), lambda j: (0, 0)),
                  pl.BlockSpec((d, MOD_TN), lambda j: (0, j)),
                  pl.BlockSpec((1, MOD_TN), lambda j: (0, j))],
        out_specs=pl.BlockSpec((batch, MOD_TN), lambda j: (0, j)),
        compiler_params=_params("arbitrary"),
        name="adaln_mod",
    )(c, w_ada, b_ada.reshape(1, n))


def _ffn_kernel(*refs, final_norm, n_cast):
    x_hbm, mod_ref, gain_ref, wg_ref, wu_ref, wo_ref, fgain_ref = refs[:7]
    cast_in = refs[7:7 + n_cast]
    o_ref = refs[7 + n_cast]
    cast_out = refs[8 + n_cast:8 + 2 * n_cast]
    y_scr, x_scr, x_sem = refs[-3:]
    i = pl.program_id(0)
    j = pl.program_id(1)
    tm = x_scr.shape[0]

    def x_copy(blk):
        rows = pl.ds(pl.multiple_of(blk * tm, tm), tm)
        return pltpu.make_async_copy(x_hbm.at[rows, :], x_scr, x_sem)

    @pl.when((i == 0) & (j == 0))
    def _():
        x_copy(0).start()

    @pl.when(j == 0)
    def _():
        x_copy(i).wait()
        _norm_modulate_rows(x_scr, gain_ref, mod_ref, y_scr)
        o_ref[...] = x_scr[...]

    @pl.when((j == 1) & (i + 1 < pl.num_programs(0)))
    def _():
        x_copy(i + 1).start()

    for src, dst in zip(cast_in, cast_out):
        dst[...] = src[...].astype(dst.dtype)

    y = y_scr[...]
    width = wg_ref.shape[1] // FFN_CHAINS
    part = None
    for n in range(FFN_CHAINS):
        sl = slice(n * width, (n + 1) * width)
        g = jnp.dot(y, wg_ref[:, sl], preferred_element_type=F32)
        u = jnp.dot(y, wu_ref[:, sl], preferred_element_type=F32)
        a = (g * jax.nn.sigmoid(g) * u).astype(BF16)
        down = jnp.dot(a, wo_ref[sl, :], preferred_element_type=F32)
        part = down if part is None else part + down
    o_ref[...] += (0.5 * mod_ref[2:3, :]) * part

    if final_norm:
        @pl.when(j == pl.num_programs(1) - 1)
        def _():
            for r in range(o_ref.shape[0] // NORM_ROWS):
                rows = slice(r * NORM_ROWS, (r + 1) * NORM_ROWS)
                h = o_ref[rows, :]
                h = h * lax.rsqrt(jnp.mean(h * h, axis=-1, keepdims=True) + EPS)
                o_ref[rows, :] = h * fgain_ref[...]


def _ffn(x2d, mod4, which, gain, w_in, w_out, fgain, seq, final_norm, to_cast=()):
    m, d = x2d.shape
    f = w_out.shape[0]
    nf = f // FFN_TF
    ni = m // FFN_TM
    blocks_per_seq = seq // FFN_TM
    assert nf >= 2, "the x refill is started in the second hidden step"
    cast_steps = min(FFN_CAST_STEPS, nf)
    cast_in, cast_out, cast_shapes = _cast_specs(
        to_cast, ni * cast_steps, lambda i, j: i * cast_steps + jnp.minimum(j, cast_steps - 1))
    outs = pl.pallas_call(
        functools.partial(_ffn_kernel, final_norm=final_norm, n_cast=len(to_cast)),
        out_shape=[jax.ShapeDtypeStruct((m, d), F32)] + cast_shapes,
        grid=(ni, nf),
        in_specs=[
            pl.BlockSpec(memory_space=pl.ANY),
            pl.BlockSpec((None, None, 3, d),
                         lambda i, j: (i // blocks_per_seq, which, 0, 0)),
            pl.BlockSpec((1, d), lambda i, j: (0, 0)),
            pl.BlockSpec((d, FFN_TF), lambda i, j: (0, j)),
            pl.BlockSpec((d, FFN_TF), lambda i, j: (0, j + nf)),
            pl.BlockSpec((FFN_TF, d), lambda i, j: (j, 0)),
            pl.BlockSpec((1, d), lambda i, j: (0, 0)),
        ] + cast_in,
        out_specs=[pl.BlockSpec((FFN_TM, d), lambda i, j: (i, 0))] + cast_out,
        scratch_shapes=[pltpu.VMEM((FFN_TM, d), BF16),
                        pltpu.VMEM((FFN_TM, d), F32),
                        pltpu.SemaphoreType.DMA(())],
        compiler_params=_params("arbitrary", "arbitrary"),
        name="ffn_final" if final_norm else "ffn",
    )(x2d, mod4, gain.reshape(1, d), w_in, w_in, w_out, fgain.reshape(1, d),
      *[w for w, _ in to_cast])
    return outs[0], outs[1:]


def _inproj_kernel(h_ref, mod_ref, gain_ref, w_ref, o_ref, y_scr):
    @pl.when(pl.program_id(1) == 0)
    def _():
        _norm_modulate_rows(h_ref, gain_ref, mod_ref, y_scr)

    o_ref[...] = jnp.dot(y_scr[...], w_ref[...],
                         preferred_element_type=F32).astype(o_ref.dtype)


def _inproj(h2d, mod4, gain, w_in, seq):
    m, d = h2d.shape
    n = w_in.shape[1]
    blocks_per_seq = seq // INPROJ_TM
    return pl.pallas_call(
        _inproj_kernel,
        out_shape=jax.ShapeDtypeStruct((m, n), BF16),
        grid=(m // INPROJ_TM, n // INPROJ_TN),
        in_specs=[
            pl.BlockSpec((INPROJ_TM, d), lambda i, j: (i, 0)),
            pl.BlockSpec((None, None, 3, d),
                         lambda i, j: (i // blocks_per_seq, 1, 0, 0)),
            pl.BlockSpec((1, d), lambda i, j: (0, 0)),
            pl.BlockSpec((d, INPROJ_TN), lambda i, j: (0, j)),
        ],
        out_specs=pl.BlockSpec((INPROJ_TM, INPROJ_TN), lambda i, j: (i, j)),
        scratch_shapes=[pltpu.VMEM((INPROJ_TM, d), BF16)],
        compiler_params=_params("arbitrary", "arbitrary"),
        name="mixer_inproj",
    )(h2d, mod4, gain.reshape(1, d), w_in)


def _attn_kernel(*refs, n_cast):
    n_side = 2 * ATTN_PAIRS
    q_refs = refs[:n_side]
    k_ref, v_ref, tri_ref = refs[n_side:n_side + 3]
    n_in = n_side + 3
    o_ref = refs[n_in + n_cast]
    q_scr, acc_scr, car_scr = refs[-3:]
    for src, dst in zip(refs[n_in:n_in + n_cast], refs[n_in + n_cast + 1:n_in + 2 * n_cast + 1]):
        dst[...] = src[...].astype(dst.dtype)

    t = ATTN_T
    nq = k_ref.shape[0] // t
    yscale = ATTN_HEAD_DIM ** -0.5 * LOG2E
    row = lax.broadcasted_iota(jnp.int32, (t, t), 0)
    col = lax.broadcasted_iota(jnp.int32, (t, t), 1)
    before = col < row

    for side in range(n_side):
        q_scr[side] = (q_refs[side][...].astype(F32) * yscale).astype(BF16)

    pairs = [pl.program_id(2) * ATTN_PAIRS + n for n in range(ATTN_PAIRS)]
    blocks = []
    for n, p in enumerate(pairs):
        blocks += [(2 * n, p, True), (2 * n + 1, nq - 1 - p, True)]
    for s in range(nq - 1):
        for n, p in enumerate(pairs):
            blocks.append((2 * n + jnp.where(s >= p, 1, 0),
                           jnp.where(s >= p, nq - 2 - s, p - 1 - s), False))

    def scores(blk):
        side, kblk, _ = blk
        k0 = pl.multiple_of(kblk * t, t)
        return lax.dot_general(q_scr[side], k_ref[pl.ds(k0, t), :],
                               (((1,), (1,)), ((), ())), preferred_element_type=F32)

    def log_keep(blk, y):
        pos = jnp.maximum(y, 0.0) + jnp.log(1.0 + jnp.exp2(-jnp.abs(y))) * LOG2E
        if blk[2]:
            pos = jnp.where(before, pos, 0.0)
        return y, jnp.dot(pos.astype(BF16), tri_ref[...], preferred_element_type=F32)

    def weights(blk, y, csum):
        side, kblk, diag = blk
        total = jnp.broadcast_to(csum[:, 0:1], (t, V7X_LANES))
        if diag:
            car_scr[side] = total
            w = jnp.where(before, jnp.exp2(y - csum), 0.0)
        else:
            carry = car_scr[side]
            car_scr[side] = carry + total
            w = jnp.exp2(y - csum - jnp.concatenate([carry, carry], axis=1))
        k0 = pl.multiple_of(kblk * t, t)
        return jnp.dot(w.astype(BF16), v_ref[pl.ds(k0, t), :], preferred_element_type=F32)

    def accumulate(blk, pv):
        if blk[2]:
            acc_scr[blk[0]] = pv
        else:
            acc_scr[blk[0]] += pv

    nb = len(blocks)
    raw, ycs, pvs = {}, {}, {}
    d1, d2, d3 = ATTN_SKEW
    for step in range(nb + d3):
        if step < nb:
            raw[step] = scores(blocks[step])
        if 0 <= step - d1 < nb:
            ycs[step - d1] = log_keep(blocks[step - d1], raw.pop(step - d1))
        if 0 <= step - d2 < nb:
            pvs[step - d2] = weights(blocks[step - d2], *ycs.pop(step - d2))
        if 0 <= step - d3 < nb:
            accumulate(blocks[step - d3], pvs.pop(step - d3))

    for side in range(n_side):
        o_ref[side // 2, side % 2] = acc_scr[side].astype(o_ref.dtype)


def _attention(proj, batch, seq, to_cast=()):
    t = ATTN_T
    nq = seq // t
    steps = nq // 2 // ATTN_PAIRS
    n_side = 2 * ATTN_PAIRS
    width = N_ATTN_HEADS * ATTN_HEAD_DIM
    idx = jnp.arange(t)
    upper = (idx[:, None] >= idx[None, :]).astype(BF16)
    cast_in, cast_out, cast_shapes = _cast_specs(
        to_cast, batch * N_ATTN_HEADS * steps,
        lambda b, h, g: (b * N_ATTN_HEADS + h) * steps + g)

    def q_spec(side):
        n, second = side // 2, side % 2

        def index(b, h, g):
            p = g * ATTN_PAIRS + n
            return (b * nq + (nq - 1 - p if second else p), h)

        return pl.BlockSpec((t, ATTN_HEAD_DIM), index)

    outs = pl.pallas_call(
        functools.partial(_attn_kernel, n_cast=len(to_cast)),
        out_shape=[jax.ShapeDtypeStruct((batch, nq // 2, 2, t, width), BF16)] + cast_shapes,
        grid=(batch, N_ATTN_HEADS, steps),
        in_specs=[q_spec(side) for side in range(n_side)] + [
            pl.BlockSpec((seq, ATTN_HEAD_DIM), lambda b, h, g: (b, N_ATTN_HEADS + h)),
            pl.BlockSpec((seq, ATTN_HEAD_DIM), lambda b, h, g: (b, 2 * N_ATTN_HEADS + h)),
            pl.BlockSpec((t, t), lambda b, h, g: (0, 0)),
        ] + cast_in,
        out_specs=[pl.BlockSpec((None, ATTN_PAIRS, 2, t, ATTN_HEAD_DIM),
                                lambda b, h, g: (b, g, 0, 0, h))] + cast_out,
        scratch_shapes=[pltpu.VMEM((n_side, t, ATTN_HEAD_DIM), BF16),
                        pltpu.VMEM((n_side, t, ATTN_HEAD_DIM), F32),
                        pltpu.VMEM((n_side, t, V7X_LANES), F32)],
        compiler_params=_params("arbitrary", "arbitrary", "arbitrary"),
        name="stickbreak_attn",
    )(*([proj] * (n_side + 2)), upper, *[w for w, _ in to_cast])
    return outs[0], outs[1:]


def _lru_kernel(xr_ref, gr_ref, cw_ref, cb_ref, wg_ref, br_ref, bi_ref, lam_ref,
                o_ref, xbuf, hcar, a_scr, u_scr):
    ts = LRU_TS
    c = xr_ref.shape[1]
    blk = c // LRU_BLOCKS
    step = pl.program_id(1)
    halo = V7X_SUBLANES

    @pl.when(step == 0)
    def _():
        xbuf[0:halo, :] = jnp.zeros((halo, c), F32)
        hcar[...] = jnp.zeros_like(hcar)

    @pl.when(step > 0)
    def _():
        xbuf[0:halo, :] = xbuf[ts:ts + halo, :]

    xbuf[halo:ts + halo, :] = xr_ref[...].astype(F32)

    xc = cb_ref[...] + cw_ref[0:1, :] * xbuf[pl.ds(halo - (CONV_WIDTH - 1), ts), :]
    for k in range(1, CONV_WIDTH):
        xc = xc + cw_ref[k:k + 1, :] * xbuf[pl.ds(halo - (CONV_WIDTH - 1) + k, ts), :]

    sp = _softplus(-lam_ref[...])
    for n in range(LRU_BLOCKS):
        sl = slice(n * blk, (n + 1) * blk)
        xn = xc[:, sl]
        gates = jnp.dot(xn.astype(BF16), wg_ref[n], preferred_element_type=F32)
        r = jax.nn.sigmoid(gates[:, :blk] + br_ref[:, sl])
        ig = jax.nn.sigmoid(gates[:, blk:] + bi_ref[:, sl])
        log_a = (-LRU_C * r) * sp[:, sl]
        a = jnp.exp(log_a)
        a_scr[:, sl] = a
        u_scr[:, sl] = jnp.sqrt(-jnp.tanh(log_a) * (1.0 + a * a)) * (ig * xn)

    row = lax.broadcasted_iota(jnp.int32, (V7X_SUBLANES, c), 0)

    def body(g, h):
        r0 = pl.multiple_of(g * V7X_SUBLANES, V7X_SUBLANES)
        a = a_scr[pl.ds(r0, V7X_SUBLANES), :]
        u = u_scr[pl.ds(r0, V7X_SUBLANES), :]
        for sh in (1, 2, 4):
            a_prev = jnp.where(row >= sh, pltpu.roll(a, sh, 0), 1.0)
            u_prev = jnp.where(row >= sh, pltpu.roll(u, sh, 0), 0.0)
            u = a * u_prev + u
            a = a * a_prev
        hh = a * h + u
        u_scr[pl.ds(r0, V7X_SUBLANES), :] = hh
        return hh[V7X_SUBLANES - 1:V7X_SUBLANES, :]

    hcar[...] = lax.fori_loop(0, ts // V7X_SUBLANES, body, hcar[...])
    o_ref[...] = (u_scr[...] * jax.nn.gelu(gr_ref[...].astype(F32))).astype(o_ref.dtype)


def _lru(proj, xr_col, conv_w, conv_b, w_gates, b_r, b_i, lam, batch, seq):
    m = proj.shape[0]
    c = conv_w.shape[1]
    nt = seq // LRU_TS
    xr_blk = xr_col // c
    return pl.pallas_call(
        _lru_kernel,
        out_shape=jax.ShapeDtypeStruct((m, c), BF16),
        grid=(batch, nt),
        in_specs=[
            pl.BlockSpec((LRU_TS, c), lambda b, s: (b * nt + s, xr_blk)),
            pl.BlockSpec((LRU_TS, c), lambda b, s: (b * nt + s, xr_blk + 1)),
            pl.BlockSpec((CONV_WIDTH, c), lambda b, s: (0, 0)),
            pl.BlockSpec((1, c), lambda b, s: (0, 0)),
            pl.BlockSpec(w_gates.shape, lambda b, s: (0, 0, 0)),
            pl.BlockSpec((1, c), lambda b, s: (0, 0)),
            pl.BlockSpec((1, c), lambda b, s: (0, 0)),
            pl.BlockSpec((1, c), lambda b, s: (0, 0)),
        ],
        out_specs=pl.BlockSpec((LRU_TS, c), lambda b, s: (b * nt + s, 0)),
        scratch_shapes=[pltpu.VMEM((LRU_TS + V7X_SUBLANES, c), F32),
                        pltpu.VMEM((1, c), F32),
                        pltpu.VMEM((LRU_TS, c), F32),
                        pltpu.VMEM((LRU_TS, c), F32)],
        compiler_params=_params("arbitrary", "arbitrary"),
        name="rg_lru",
    )(proj, proj, conv_w, conv_b.reshape(1, c), w_gates, b_r.reshape(1, c),
      b_i.reshape(1, c), lam.reshape(1, c))


def _merge_kernel(h_ref, ya_ref, yl_ref, ga0_ref, ga1_ref, gl0_ref, gl1_ref, mod_ref,
                  wa_ref, wl_ref, wo_ref, o_ref):
    pa = jnp.dot(ya_ref[...], wa_ref[...], preferred_element_type=F32)
    plru = jnp.dot(yl_ref[...], wl_ref[...], preferred_element_type=F32)
    ga = jnp.concatenate([ga0_ref[...], ga1_ref[...]], axis=1).astype(F32)
    gl = jnp.concatenate([gl0_ref[...], gl1_ref[...]], axis=1).astype(F32)
    merged = jax.nn.sigmoid(ga) * pa + jax.nn.sigmoid(gl) * plru
    out = jnp.dot(merged.astype(BF16), wo_ref[...], preferred_element_type=F32)
    o_ref[...] = h_ref[...] + mod_ref[2:3, :] * out


def _merge(h2d, y_attn, y_lru, proj, gate_col, mod4, w_ba, w_bl, w_o, seq):
    m, d = h2d.shape
    tm = ATTN_T
    nq = seq // tm
    wa = y_attn.shape[-1]
    wl = y_lru.shape[1]
    half = d // 2
    g0 = gate_col // half
    resident = pl.Buffered(1)

    def attn_block(i):
        ii = i % nq
        first = ii < nq // 2
        return (i // nq, jnp.where(first, ii, nq - 1 - ii), jnp.where(first, 0, 1), 0, 0)

    return pl.pallas_call(
        _merge_kernel,
        out_shape=jax.ShapeDtypeStruct((m, d), F32),
        grid=(m // tm,),
        in_specs=[
            pl.BlockSpec((tm, d), lambda i: (i, 0)),
            pl.BlockSpec((None, None, None, tm, wa), attn_block),
            pl.BlockSpec((tm, wl), lambda i: (i, 0)),
            pl.BlockSpec((tm, half), lambda i: (i, g0)),
            pl.BlockSpec((tm, half), lambda i: (i, g0 + 1)),
            pl.BlockSpec((tm, half), lambda i: (i, g0 + 2)),
            pl.BlockSpec((tm, half), lambda i: (i, g0 + 3)),
            pl.BlockSpec((None, None, 3, d), lambda i: (i // nq, 1, 0, 0)),
            pl.BlockSpec((wa, d), lambda i: (0, 0), pipeline_mode=resident),
            pl.BlockSpec((wl, d), lambda i: (0, 0), pipeline_mode=resident),
            pl.BlockSpec((d, d), lambda i: (0, 0), pipeline_mode=resident),
        ],
        out_specs=pl.BlockSpec((tm, d), lambda i: (i, 0)),
        compiler_params=_params("arbitrary"),
        name="merge_outproj",
    )(h2d, y_attn, y_lru, proj, proj, proj, proj, mod4, w_ba, w_bl, w_o)


def _mixer_out_kernel(h_ref, ya_ref, xr0_ref, gr0_ref, xr_ref, gr_ref, ga0_ref, ga1_ref,
                      gl0_ref, gl1_ref, mod_ref, cw_ref, cb_ref, wg_ref, br_ref, bi_ref,
                      lam_ref, wa_ref, wl_ref, wo_ref, o_ref, xbuf, hcar, a_scr, u_scr,
                      ylru_scr, *, blocks_per_seq):
    i = pl.program_id(0)
    ts = xr_ref.shape[0]
    c = xr_ref.shape[1]
    blk = c // LRU_BLOCKS
    halo = V7X_SUBLANES
    row = lax.broadcasted_iota(jnp.int32, (V7X_SUBLANES, c), 0)

    def start_block(first):
        @pl.when(first)
        def _():
            xbuf[0:halo, :] = jnp.zeros((halo, c), F32)
            hcar[...] = jnp.zeros_like(hcar)

        @pl.when(jnp.logical_not(first))
        def _():
            xbuf[0:halo, :] = xbuf[ts:ts + halo, :]

    def conv(x_ref):
        xbuf[halo:ts + halo, :] = x_ref[...].astype(F32)
        xc = cb_ref[...] + cw_ref[0:1, :] * xbuf[pl.ds(halo - (CONV_WIDTH - 1), ts), :]
        for k in range(1, CONV_WIDTH):
            xc = xc + cw_ref[k:k + 1, :] * xbuf[pl.ds(halo - (CONV_WIDTH - 1) + k, ts), :]
        return xc

    def gate_blocks(xc, blocks):
        sp = _softplus(-lam_ref[...])
        for n in blocks:
            sl = slice(n * blk, (n + 1) * blk)
            xn = xc[:, sl]
            gates = jnp.dot(xn.astype(BF16), wg_ref[n], preferred_element_type=F32)
            r = jax.nn.sigmoid(gates[:, :blk] + br_ref[:, sl])
            ig = jax.nn.sigmoid(gates[:, blk:] + bi_ref[:, sl])
            log_a = (-LRU_C * r) * sp[:, sl]
            a = jnp.exp(log_a)
            a_scr[:, sl] = a
            u_scr[:, sl] = jnp.sqrt(-jnp.tanh(log_a) * (1.0 + a * a)) * (ig * xn)

    def scan_rows(state, g_ref, slot, row0, nrows):
        for g in range(row0 // V7X_SUBLANES, (row0 + nrows) // V7X_SUBLANES):
            rows = slice(g * V7X_SUBLANES, (g + 1) * V7X_SUBLANES)
            a = a_scr[rows, :]
            u = u_scr[rows, :]
            for sh in (1, 2, 4):
                a_prev = jnp.where(row >= sh, pltpu.roll(a, sh, 0), 1.0)
                u_prev = jnp.where(row >= sh, pltpu.roll(u, sh, 0), 0.0)
                u = a * u_prev + u
                a = a * a_prev
            hh = a * state + u
            u_scr[rows, :] = hh
            state = hh[V7X_SUBLANES - 1:V7X_SUBLANES, :]
        rows = slice(row0, row0 + nrows)
        ylru_scr[slot, rows, :] = (u_scr[rows, :] * jax.nn.gelu(g_ref[rows, :].astype(F32))
                                   ).astype(BF16)
        return state

    @pl.when(i == 0)
    def _():
        xbuf[0:halo, :] = jnp.zeros((halo, c), F32)
        hcar[...] = jnp.zeros_like(hcar)
        gate_blocks(conv(xr0_ref), range(LRU_BLOCKS))
        hcar[...] = scan_rows(hcar[...], gr0_ref, 0, 0, ts)

    nxt = jnp.minimum(i + 1, pl.num_programs(0) - 1)
    start_block(nxt % blocks_per_seq == 0)

    d = o_ref.shape[1]
    n_slices = LRU_BLOCKS
    width = d // n_slices
    gate_refs = ((ga0_ref, gl0_ref), (ga1_ref, gl1_ref))
    per_ref = n_slices // 2
    xc = conv(xr_ref)
    ya = ya_ref[...]
    yl = ylru_scr[i % 2]
    merged = []
    for k in range(n_slices):
        cols = slice(k * width, (k + 1) * width)
        pa = jnp.dot(ya, wa_ref[:, cols], preferred_element_type=F32)
        plru = jnp.dot(yl, wl_ref[:, cols], preferred_element_type=F32)
        gate_blocks(xc, [k])
        ga_ref, gl_ref = gate_refs[k // per_ref]
        gcols = slice((k % per_ref) * width, (k % per_ref + 1) * width)
        merged.append((jax.nn.sigmoid(ga_ref[:, gcols].astype(F32)) * pa
                       + jax.nn.sigmoid(gl_ref[:, gcols].astype(F32)) * plru).astype(BF16))
    merged = jnp.concatenate(merged, axis=1)

    state = hcar[...]
    rows_per = ts // n_slices
    for k in range(n_slices):
        cols = slice(k * width, (k + 1) * width)
        out = jnp.dot(merged, wo_ref[:, cols], preferred_element_type=F32)
        o_ref[:, cols] = h_ref[:, cols] + mod_ref[2:3, cols] * out
        state = scan_rows(state, gr_ref, (i + 1) % 2, k * rows_per, rows_per)
    hcar[...] = state


def _mixer_out(h2d, y_attn, proj, xr_col, mod4, conv_w, conv_b, w_gates, b_r, b_i, lam,
               w_ba, w_bl, w_o, seq):
    m, d = h2d.shape
    tm = ATTN_T
    nq = seq // tm
    wa = y_attn.shape[-1]
    c = conv_w.shape[1]
    half = d // 2
    xr_blk = xr_col // c
    g0 = (xr_col + 2 * c) // half
    n_blocks = m // tm
    resident = pl.Buffered(1)

    def attn_block(i):
        ii = i % nq
        first = ii < nq // 2
        return (i // nq, jnp.where(first, ii, nq - 1 - ii), jnp.where(first, 0, 1), 0, 0)

    def whole(shape):
        return pl.BlockSpec(shape, lambda i: (0,) * len(shape), pipeline_mode=resident)

    return pl.pallas_call(
        functools.partial(_mixer_out_kernel, blocks_per_seq=nq),
        out_shape=jax.ShapeDtypeStruct((m, d), F32),
        grid=(m // tm,),
        in_specs=[
            pl.BlockSpec((tm, d), lambda i: (i, 0)),
            pl.BlockSpec((None, None, None, tm, wa), attn_block),
            pl.BlockSpec((tm, c), lambda i: (0, xr_blk)),
            pl.BlockSpec((tm, c), lambda i: (0, xr_blk + 1)),
            pl.BlockSpec((tm, c), lambda i: (jnp.minimum(i + 1, n_blocks - 1), xr_blk)),
            pl.BlockSpec((tm, c), lambda i: (jnp.minimum(i + 1, n_blocks - 1), xr_blk + 1)),
            pl.BlockSpec((tm, half), lambda i: (i, g0)),
            pl.BlockSpec((tm, half), lambda i: (i, g0 + 1)),
            pl.BlockSpec((tm, half), lambda i: (i, g0 + 2)),
            pl.BlockSpec((tm, half), lambda i: (i, g0 + 3)),
            pl.BlockSpec((None, None, 3, d), lambda i: (i // nq, 1, 0, 0)),
            whole((CONV_WIDTH, c)),
            whole((1, c)),
            whole(w_gates.shape),
            whole((1, c)),
            whole((1, c)),
            whole((1, c)),
            whole((wa, d)),
            whole((c, d)),
            whole((d, d)),
        ],
        out_specs=pl.BlockSpec((tm, d), lambda i: (i, 0)),
        scratch_shapes=[pltpu.VMEM((tm + V7X_SUBLANES, c), F32),
                        pltpu.VMEM((1, c), F32),
                        pltpu.VMEM((tm, c), F32),
                        pltpu.VMEM((tm, c), F32),
                        pltpu.VMEM((2, tm, c), BF16)],
        compiler_params=_params("arbitrary"),
        name="lru_merge_outproj",
    )(h2d, y_attn, proj, proj, proj, proj, proj, proj, proj, proj, mod4, conv_w,
      conv_b.reshape(1, c),
      w_gates, b_r.reshape(1, c), b_i.reshape(1, c), lam.reshape(1, c), w_ba, w_bl, w_o)


def kernel(x, c, w_ada, b_ada, norm_ffn1, w_ffn1_in, w_ffn1_out, norm_mix, w_in,
           conv_w, conv_b, w_rg_gate, b_rg_gate, w_in_gate, b_in_gate, lru_lambda,
           w_branch_attn, w_branch_lru, w_out, norm_ffn2, w_ffn2_in, w_ffn2_out,
           norm_final):
    batch, seq, d = x.shape
    depth = w_ada.shape[0]
    assert depth >= 1, "the final RMSNorm is fused into the last layer"
    attn_width = N_ATTN_HEADS * ATTN_HEAD_DIM
    lru_width = conv_w.shape[-1]
    xr_col = 3 * attn_width
    gate_col = xr_col + 2 * lru_width
    h = x.reshape(batch * seq, d)
    ones = jnp.ones((d,), F32)
    for l in range(depth):
        last = l == depth - 1
        mod = _modulation(c, w_ada[l], b_ada[l])
        mod4 = mod.reshape(batch, N_MOD // 3, 3, d)

        h, (w_in_bf,) = _ffn(h, mod4, 0, norm_ffn1[l], w_ffn1_in[l].astype(BF16),
                             w_ffn1_out[l].astype(BF16), ones, seq, False,
                             to_cast=((w_in, l),))

        proj = _inproj(h, mod4, norm_mix[l], w_in_bf, seq)
        y_attn, (w_ba, w_bl, w_o, w_f2_in, w_f2_out) = _attention(
            proj, batch, seq,
            to_cast=((w_branch_attn, l), (w_branch_lru, l), (w_out, l), (w_ffn2_in, l),
                     (w_ffn2_out, l)))
        w_gates = jnp.concatenate([w_rg_gate[l], w_in_gate[l]], axis=-1).astype(BF16)
        h = _mixer_out(h, y_attn, proj, xr_col, mod4, conv_w[l], conv_b[l], w_gates,
                       b_rg_gate[l], b_in_gate[l], lru_lambda[l], w_ba, w_bl, w_o, seq)

        h, _ = _ffn(h, mod4, 2, norm_ffn2[l], w_f2_in, w_f2_out,
                    norm_final if last else ones, seq, last)
    return h.reshape(batch, seq, d)
```

```python
import functools

import jax
import jax.numpy as jnp
from jax import lax
from jax.experimental import pallas as pl
from jax.experimental.pallas import tpu as pltpu

F32 = jnp.float32
BF16 = jnp.bfloat16

EPS = 1e-6
N_MOD = 9
N_ATTN_HEADS = 8
ATTN_HEAD_DIM = 128
LRU_BLOCKS = 8
CONV_WIDTH = 4
LRU_C = 8.0
LOG2E = 1.4426950408889634

V7X_LANES = 128
V7X_SUBLANES = 8
V7X_VMEM_LIMIT_BYTES = 56 * 1024 * 1024

MOD_TN = 1024
FFN_TM = 1024
FFN_TF = 512
FFN_CAST_STEPS = 8
FFN_CHAINS = 2
INPROJ_TM = 1024
INPROJ_TN = 1024
ATTN_T = 256
ATTN_PAIRS = 8
ATTN_SKEW = (2, 3, 4)
LRU_TS = 512
NORM_UNROLL = 8
NORM_ROWS = 16


def _params(*sem):
    return pltpu.CompilerParams(dimension_semantics=sem,
                                vmem_limit_bytes=V7X_VMEM_LIMIT_BYTES)


def _softplus(x):
    return jnp.maximum(x, 0.0) + jnp.log(1.0 + jnp.exp(-jnp.abs(x)))


def _norm_modulate_rows(x_ref, gain_ref, mod_ref, y_ref):
    gain_scale = gain_ref[...] * (1.0 + mod_ref[1:2, :])
    shift = mod_ref[0:1, :]

    def body(r, carry):
        rows = pl.ds(pl.multiple_of(r * NORM_ROWS, NORM_ROWS), NORM_ROWS)
        x = x_ref[rows, :]
        y = x * lax.rsqrt(jnp.mean(x * x, axis=-1, keepdims=True) + EPS)
        y_ref[rows, :] = (y * gain_scale + shift).astype(y_ref.dtype)
        return carry

    lax.fori_loop(0, x_ref.shape[0] // NORM_ROWS, body, 0, unroll=NORM_UNROLL)


def _cast_block(shape, nsteps):
    r, c = shape
    row_tile = 2 * V7X_SUBLANES
    best = None
    for row_blocks in range(1, nsteps + 1):
        if nsteps % row_blocks or r % row_blocks or (r // row_blocks) % row_tile:
            continue
        col_blocks = nsteps // row_blocks
        if c % col_blocks or (c // col_blocks) % V7X_LANES:
            continue
        if best is None or c // col_blocks > best[1]:
            best = (r // row_blocks, c // col_blocks)
    assert best is not None, (shape, nsteps)
    return best


def _cast_specs(to_cast, nsteps, step_of):
    in_specs, out_specs, out_shapes = [], [], []
    for w, layer in to_cast:
        rows, cols = _cast_block(w.shape[1:], nsteps)
        col_blocks = w.shape[2] // cols

        def in_map(*idx, layer=layer, col_blocks=col_blocks):
            s = step_of(*idx)
            return (layer, s // col_blocks, s % col_blocks)

        def out_map(*idx, col_blocks=col_blocks):
            s = step_of(*idx)
            return (s // col_blocks, s % col_blocks)

        in_specs.append(pl.BlockSpec((None, rows, cols), in_map))
        out_specs.append(pl.BlockSpec((rows, cols), out_map))
        out_shapes.append(jax.ShapeDtypeStruct(w.shape[1:], BF16))
    return in_specs, out_specs, out_shapes


def _mod_kernel(c_ref, w_ref, b_ref, o_ref):
    c = c_ref[...]
    c_act = (c * jax.nn.sigmoid(c)).astype(BF16)
    o_ref[...] = jnp.dot(c_act, w_ref[...].astype(BF16),
                         preferred_element_type=F32) + b_ref[...]


def _modulation(c, w_ada, b_ada):
    batch, d = c.shape
    n = w_ada.shape[1]
    return pl.pallas_call(
        _mod_kernel,
        out_shape=jax.ShapeDtypeStruct((batch, n), F32),
        grid=(n // MOD_TN,),
        in_specs=[pl.BlockSpec((batch, d), lambda j: (0, 0)),
                  pl.BlockSpec((d, MOD_TN), lambda j: (0, j)),
                  pl.BlockSpec((1, MOD_TN), lambda j: (0, j))],
        out_specs=pl.BlockSpec((batch, MOD_TN), lambda j: (0, j)),
        compiler_params=_params("arbitrary"),
        name="adaln_mod",
    )(c, w_ada, b_ada.reshape(1, n))


def _ffn_kernel(*refs, final_norm, n_cast):
    x_hbm, mod_ref, gain_ref, wg_ref, wu_ref, wo_ref, fgain_ref = refs[:7]
    cast_in = refs[7:7 + n_cast]
    o_ref = refs[7 + n_cast]
    cast_out = refs[8 + n_cast:8 + 2 * n_cast]
    y_scr, x_scr, x_sem = refs[-3:]
    i = pl.program_id(0)
    j = pl.program_id(1)
    tm = x_scr.shape[0]

    def x_copy(blk):
        rows = pl.ds(pl.multiple_of(blk * tm, tm), tm)
        return pltpu.make_async_copy(x_hbm.at[rows, :], x_scr, x_sem)

    @pl.when((i == 0) & (j == 0))
    def _():
        x_copy(0).start()

    @pl.when(j == 0)
    def _():
        x_copy(i).wait()
        _norm_modulate_rows(x_scr, gain_ref, mod_ref, y_scr)
        o_ref[...] = x_scr[...]

    @pl.when((j == 1) & (i + 1 < pl.num_programs(0)))
    def _():
        x_copy(i + 1).start()

    for src, dst in zip(cast_in, cast_out):
        dst[...] = src[...].astype(dst.dtype)

    y = y_scr[...]
    width = wg_ref.shape[1] // FFN_CHAINS
    part = None
    for n in range(FFN_CHAINS):
        sl = slice(n * width, (n + 1) * width)
        g = jnp.dot(y, wg_ref[:, sl], preferred_element_type=F32)
        u = jnp.dot(y, wu_ref[:, sl], preferred_element_type=F32)
        a = (g * jax.nn.sigmoid(g) * u).astype(BF16)
        down = jnp.dot(a, wo_ref[sl, :], preferred_element_type=F32)
        part = down if part is None else part + down
    o_ref[...] += (0.5 * mod_ref[2:3, :]) * part

    if final_norm:
        @pl.when(j == pl.num_programs(1) - 1)
        def _():
            for r in range(o_ref.shape[0] // NORM_ROWS):
                rows = slice(r * NORM_ROWS, (r + 1) * NORM_ROWS)
                h = o_ref[rows, :]
                h = h * lax.rsqrt(jnp.mean(h * h, axis=-1, keepdims=True) + EPS)
                o_ref[rows, :] = h * fgain_ref[...]


def _ffn(x2d, mod4, which, gain, w_in, w_out, fgain, seq, final_norm, to_cast=()):
    m, d = x2d.shape
    f = w_out.shape[0]
    nf = f // FFN_TF
    ni = m // FFN_TM
    blocks_per_seq = seq // FFN_TM
    assert nf >= 2, "the x refill is started in the second hidden step"
    cast_steps = min(FFN_CAST_STEPS, nf)
    cast_in, cast_out, cast_shapes = _cast_specs(
        to_cast, ni * cast_steps, lambda i, j: i * cast_steps + jnp.minimum(j, cast_steps - 1))
    outs = pl.pallas_call(
        functools.partial(_ffn_kernel, final_norm=final_norm, n_cast=len(to_cast)),
        out_shape=[jax.ShapeDtypeStruct((m, d), F32)] + cast_shapes,
        grid=(ni, nf),
        in_specs=[
            pl.BlockSpec(memory_space=pl.ANY),
            pl.BlockSpec((None, None, 3, d),
                         lambda i, j: (i // blocks_per_seq, which, 0, 0)),
            pl.BlockSpec((1, d), lambda i, j: (0, 0)),
            pl.BlockSpec((d, FFN_TF), lambda i, j: (0, j)),
            pl.BlockSpec((d, FFN_TF), lambda i, j: (0, j + nf)),
            pl.BlockSpec((FFN_TF, d), lambda i, j: (j, 0)),
            pl.BlockSpec((1, d), lambda i, j: (0, 0)),
        ] + cast_in,
        out_specs=[pl.BlockSpec((FFN_TM, d), lambda i, j: (i, 0))] + cast_out,
        scratch_shapes=[pltpu.VMEM((FFN_TM, d), BF16),
                        pltpu.VMEM((FFN_TM, d), F32),
                        pltpu.SemaphoreType.DMA(())],
        compiler_params=_params("arbitrary", "arbitrary"),
        name="ffn_final" if final_norm else "ffn",
    )(x2d, mod4, gain.reshape(1, d), w_in, w_in, w_out, fgain.reshape(1, d),
      *[w for w, _ in to_cast])
    return outs[0], outs[1:]


def _inproj_kernel(h_ref, mod_ref, gain_ref, w_ref, o_ref, y_scr):
    @pl.when(pl.program_id(1) == 0)
    def _():
        _norm_modulate_rows(h_ref, gain_ref, mod_ref, y_scr)

    o_ref[...] = jnp.dot(y_scr[...], w_ref[...],
                         preferred_element_type=F32).astype(o_ref.dtype)


def _inproj(h2d, mod4, gain, w_in, seq):
    m, d = h2d.shape
    n = w_in.shape[1]
    blocks_per_seq = seq // INPROJ_TM
    return pl.pallas_call(
        _inproj_kernel,
        out_shape=jax.ShapeDtypeStruct((m, n), BF16),
        grid=(m // INPROJ_TM, n // INPROJ_TN),
        in_specs=[
            pl.BlockSpec((INPROJ_TM, d), lambda i, j: (i, 0)),
            pl.BlockSpec((None, None, 3, d),
                         lambda i, j: (i // blocks_per_seq, 1, 0, 0)),
            pl.BlockSpec((1, d), lambda i, j: (0, 0)),
            pl.BlockSpec((d, INPROJ_TN), lambda i, j: (0, j)),
        ],
        out_specs=pl.BlockSpec((INPROJ_TM, INPROJ_TN), lambda i, j: (i, j)),
        scratch_shapes=[pltpu.VMEM((INPROJ_TM, d), BF16)],
        compiler_params=_params("arbitrary", "arbitrary"),
        name="mixer_inproj",
    )(h2d, mod4, gain.reshape(1, d), w_in)


def _attn_kernel(*refs, n_cast):
    n_side = 2 * ATTN_PAIRS
    q_refs = refs[:n_side]
    k_ref, v_ref, tri_ref = refs[n_side:n_side + 3]
    n_in = n_side + 3
    o_ref = refs[n_in + n_cast]
    q_scr, acc_scr, car_scr = refs[-3:]
    for src, dst in zip(refs[n_in:n_in + n_cast], refs[n_in + n_cast + 1:n_in + 2 * n_cast + 1]):
        dst[...] = src[...].astype(dst.dtype)

    t = ATTN_T
    nq = k_ref.shape[0] // t
    yscale = ATTN_HEAD_DIM ** -0.5 * LOG2E
    row = lax.broadcasted_iota(jnp.int32, (t, t), 0)
    col = lax.broadcasted_iota(jnp.int32, (t, t), 1)
    before = col < row

    for side in range(n_side):
        q_scr[side] = (q_refs[side][...].astype(F32) * yscale).astype(BF16)

    pairs = [pl.program_id(2) * ATTN_PAIRS + n for n in range(ATTN_PAIRS)]
    blocks = []
    for n, p in enumerate(pairs):
        blocks += [(2 * n, p, True), (2 * n + 1, nq - 1 - p, True)]
    for s in range(nq - 1):
        for n, p in enumerate(pairs):
            blocks.append((2 * n + jnp.where(s >= p, 1, 0),
                           jnp.where(s >= p, nq - 2 - s, p - 1 - s), False))

    def scores(blk):
        side, kblk, _ = blk
        k0 = pl.multiple_of(kblk * t, t)
        return lax.dot_general(q_scr[side], k_ref[pl.ds(k0, t), :],
                               (((1,), (1,)), ((), ())), preferred_element_type=F32)

    def log_keep(blk, y):
        pos = jnp.maximum(y, 0.0) + jnp.log(1.0 + jnp.exp2(-jnp.abs(y))) * LOG2E
        if blk[2]:
            pos = jnp.where(before, pos, 0.0)
        return y, jnp.dot(pos.astype(BF16), tri_ref[...], preferred_element_type=F32)

    def weights(blk, y, csum):
        side, kblk, diag = blk
        total = jnp.broadcast_to(csum[:, 0:1], (t, V7X_LANES))
        if diag:
            car_scr[side] = total
            w = jnp.where(before, jnp.exp2(y - csum), 0.0)
        else:
            carry = car_scr[side]
            car_scr[side] = carry + total
            w = jnp.exp2(y - csum - jnp.concatenate([carry, carry], axis=1))
        k0 = pl.multiple_of(kblk * t, t)
        return jnp.dot(w.astype(BF16), v_ref[pl.ds(k0, t), :], preferred_element_type=F32)

    def accumulate(blk, pv):
        if blk[2]:
            acc_scr[blk[0]] = pv
        else:
            acc_scr[blk[0]] += pv

    nb = len(blocks)
    raw, ycs, pvs = {}, {}, {}
    d1, d2, d3 = ATTN_SKEW
    for step in range(nb + d3):
        if step < nb:
            raw[step] = scores(blocks[step])
        if 0 <= step - d1 < nb:
            ycs[step - d1] = log_keep(blocks[step - d1], raw.pop(step - d1))
        if 0 <= step - d2 < nb:
            pvs[step - d2] = weights(blocks[step - d2], *ycs.pop(step - d2))
        if 0 <= step - d3 < nb:
            accumulate(blocks[step - d3], pvs.pop(step - d3))

    for side in range(n_side):
        o_ref[side // 2, side % 2] = acc_scr[side].astype(o_ref.dtype)


def _attention(proj, batch, seq, to_cast=()):
    t = ATTN_T
    nq = seq // t
    steps = nq // 2 // ATTN_PAIRS
    n_side = 2 * ATTN_PAIRS
    width = N_ATTN_HEADS * ATTN_HEAD_DIM
    idx = jnp.arange(t)
    upper = (idx[:, None] >= idx[None, :]).astype(BF16)
    cast_in, cast_out, cast_shapes = _cast_specs(
        to_cast, batch * N_ATTN_HEADS * steps,
        lambda b, h, g: (b * N_ATTN_HEADS + h) * steps + g)

    def q_spec(side):
        n, second = side // 2, side % 2

        def index(b, h, g):
            p = g * ATTN_PAIRS + n
            return (b * nq + (nq - 1 - p if second else p), h)

        return pl.BlockSpec((t, ATTN_HEAD_DIM), index)

    outs = pl.pallas_call(
        functools.partial(_attn_kernel, n_cast=len(to_cast)),
        out_shape=[jax.ShapeDtypeStruct((batch, nq // 2, 2, t, width), BF16)] + cast_shapes,
        grid=(batch, N_ATTN_HEADS, steps),
        in_specs=[q_spec(side) for side in range(n_side)] + [
            pl.BlockSpec((seq, ATTN_HEAD_DIM), lambda b, h, g: (b, N_ATTN_HEADS + h)),
            pl.BlockSpec((seq, ATTN_HEAD_DIM), lambda b, h, g: (b, 2 * N_ATTN_HEADS + h)),
            pl.BlockSpec((t, t), lambda b, h, g: (0, 0)),
        ] + cast_in,
        out_specs=[pl.BlockSpec((None, ATTN_PAIRS, 2, t, ATTN_HEAD_DIM),
                                lambda b, h, g: (b, g, 0, 0, h))] + cast_out,
        scratch_shapes=[pltpu.VMEM((n_side, t, ATTN_HEAD_DIM), BF16),
                        pltpu.VMEM((n_side, t, ATTN_HEAD_DIM), F32),
                        pltpu.VMEM((n_side, t, V7X_LANES), F32)],
        compiler_params=_params("arbitrary", "arbitrary", "arbitrary"),
        name="stickbreak_attn",
    )(*([proj] * (n_side + 2)), upper, *[w for w, _ in to_cast])
    return outs[0], outs[1:]


def _lru_kernel(xr_ref, gr_ref, cw_ref, cb_ref, wg_ref, br_ref, bi_ref, lam_ref,
                o_ref, xbuf, hcar, a_scr, u_scr):
    ts = LRU_TS
    c = xr_ref.shape[1]
    blk = c // LRU_BLOCKS
    step = pl.program_id(1)
    halo = V7X_SUBLANES

    @pl.when(step == 0)
    def _():
        xbuf[0:halo, :] = jnp.zeros((halo, c), F32)
        hcar[...] = jnp.zeros_like(hcar)

    @pl.when(step > 0)
    def _():
        xbuf[0:halo, :] = xbuf[ts:ts + halo, :]

    xbuf[halo:ts + halo, :] = xr_ref[...].astype(F32)

    xc = cb_ref[...] + cw_ref[0:1, :] * xbuf[pl.ds(halo - (CONV_WIDTH - 1), ts), :]
    for k in range(1, CONV_WIDTH):
        xc = xc + cw_ref[k:k + 1, :] * xbuf[pl.ds(halo - (CONV_WIDTH - 1) + k, ts), :]

    sp = _softplus(-lam_ref[...])
    for n in range(LRU_BLOCKS):
        sl = slice(n * blk, (n + 1) * blk)
        xn = xc[:, sl]
        gates = jnp.dot(xn.astype(BF16), wg_ref[n], preferred_element_type=F32)
        r = jax.nn.sigmoid(gates[:, :blk] + br_ref[:, sl])
        ig = jax.nn.sigmoid(gates[:, blk:] + bi_ref[:, sl])
        log_a = (-LRU_C * r) * sp[:, sl]
        a = jnp.exp(log_a)
        a_scr[:, sl] = a
        u_scr[:, sl] = jnp.sqrt(-jnp.tanh(log_a) * (1.0 + a * a)) * (ig * xn)

    row = lax.broadcasted_iota(jnp.int32, (V7X_SUBLANES, c), 0)

    def body(g, h):
        r0 = pl.multiple_of(g * V7X_SUBLANES, V7X_SUBLANES)
        a = a_scr[pl.ds(r0, V7X_SUBLANES), :]
        u = u_scr[pl.ds(r0, V7X_SUBLANES), :]
        for sh in (1, 2, 4):
            a_prev = jnp.where(row >= sh, pltpu.roll(a, sh, 0), 1.0)
            u_prev = jnp.where(row >= sh, pltpu.roll(u, sh, 0), 0.0)
            u = a * u_prev + u
            a = a * a_prev
        hh = a * h + u
        u_scr[pl.ds(r0, V7X_SUBLANES), :] = hh
        return hh[V7X_SUBLANES - 1:V7X_SUBLANES, :]

    hcar[...] = lax.fori_loop(0, ts // V7X_SUBLANES, body, hcar[...])
    o_ref[...] = (u_scr[...] * jax.nn.gelu(gr_ref[...].astype(F32))).astype(o_ref.dtype)


def _lru(proj, xr_col, conv_w, conv_b, w_gates, b_r, b_i, lam, batch, seq):
    m = proj.shape[0]
    c = conv_w.shape[1]
    nt = seq // LRU_TS
    xr_blk = xr_col // c
    return pl.pallas_call(
        _lru_kernel,
        out_shape=jax.ShapeDtypeStruct((m, c), BF16),
        grid=(batch, nt),
        in_specs=[
            pl.BlockSpec((LRU_TS, c), lambda b, s: (b * nt + s, xr_blk)),
            pl.BlockSpec((LRU_TS, c), lambda b, s: (b * nt + s, xr_blk + 1)),
            pl.BlockSpec((CONV_WIDTH, c), lambda b, s: (0, 0)),
            pl.BlockSpec((1, c), lambda b, s: (0, 0)),
            pl.BlockSpec(w_gates.shape, lambda b, s: (0, 0, 0)),
            pl.BlockSpec((1, c), lambda b, s: (0, 0)),
            pl.BlockSpec((1, c), lambda b, s: (0, 0)),
            pl.BlockSpec((1, c), lambda b, s: (0, 0)),
        ],
        out_specs=pl.BlockSpec((LRU_TS, c), lambda b, s: (b * nt + s, 0)),
        scratch_shapes=[pltpu.VMEM((LRU_TS + V7X_SUBLANES, c), F32),
                        pltpu.VMEM((1, c), F32),
                        pltpu.VMEM((LRU_TS, c), F32),
                        pltpu.VMEM((LRU_TS, c), F32)],
        compiler_params=_params("arbitrary", "arbitrary"),
        name="rg_lru",
    )(proj, proj, conv_w, conv_b.reshape(1, c), w_gates, b_r.reshape(1, c),
      b_i.reshape(1, c), lam.reshape(1, c))


def _merge_kernel(h_ref, ya_ref, yl_ref, ga0_ref, ga1_ref, gl0_ref, gl1_ref, mod_ref,
                  wa_ref, wl_ref, wo_ref, o_ref):
    pa = jnp.dot(ya_ref[...], wa_ref[...], preferred_element_type=F32)
    plru = jnp.dot(yl_ref[...], wl_ref[...], preferred_element_type=F32)
    ga = jnp.concatenate([ga0_ref[...], ga1_ref[...]], axis=1).astype(F32)
    gl = jnp.concatenate([gl0_ref[...], gl1_ref[...]], axis=1).astype(F32)
    merged = jax.nn.sigmoid(ga) * pa + jax.nn.sigmoid(gl) * plru
    out = jnp.dot(merged.astype(BF16), wo_ref[...], preferred_element_type=F32)
    o_ref[...] = h_ref[...] + mod_ref[2:3, :] * out


def _merge(h2d, y_attn, y_lru, proj, gate_col, mod4, w_ba, w_bl, w_o, seq):
    m, d = h2d.shape
    tm = ATTN_T
    nq = seq // tm
    wa = y_attn.shape[-1]
    wl = y_lru.shape[1]
    half = d // 2
    g0 = gate_col // half
    resident = pl.Buffered(1)

    def attn_block(i):
        ii = i % nq
        first = ii < nq // 2
        return (i // nq, jnp.where(first, ii, nq - 1 - ii), jnp.where(first, 0, 1), 0, 0)

    return pl.pallas_call(
        _merge_kernel,
        out_shape=jax.ShapeDtypeStruct((m, d), F32),
        grid=(m // tm,),
        in_specs=[
            pl.BlockSpec((tm, d), lambda i: (i, 0)),
            pl.BlockSpec((None, None, None, tm, wa), attn_block),
            pl.BlockSpec((tm, wl), lambda i: (i, 0)),
            pl.BlockSpec((tm, half), lambda i: (i, g0)),
            pl.BlockSpec((tm, half), lambda i: (i, g0 + 1)),
            pl.BlockSpec((tm, half), lambda i: (i, g0 + 2)),
            pl.BlockSpec((tm, half), lambda i: (i, g0 + 3)),
            pl.BlockSpec((None, None, 3, d), lambda i: (i // nq, 1, 0, 0)),
            pl.BlockSpec((wa, d), lambda i: (0, 0), pipeline_mode=resident),
            pl.BlockSpec((wl, d), lambda i: (0, 0), pipeline_mode=resident),
            pl.BlockSpec((d, d), lambda i: (0, 0), pipeline_mode=resident),
        ],
        out_specs=pl.BlockSpec((tm, d), lambda i: (i, 0)),
        compiler_params=_params("arbitrary"),
        name="merge_outproj",
    )(h2d, y_attn, y_lru, proj, proj, proj, proj, mod4, w_ba, w_bl, w_o)


def _mixer_out_kernel(h_ref, ya_ref, xr0_ref, gr0_ref, xr_ref, gr_ref, ga0_ref, ga1_ref,
                      gl0_ref, gl1_ref, mod_ref, cw_ref, cb_ref, wg_ref, br_ref, bi_ref,
                      lam_ref, wa_ref, wl_ref, wo_ref, o_ref, xbuf, hcar, a_scr, u_scr,
                      ylru_scr, *, blocks_per_seq):
    i = pl.program_id(0)
    ts = xr_ref.shape[0]
    c = xr_ref.shape[1]
    blk = c // LRU_BLOCKS
    halo = V7X_SUBLANES
    row = lax.broadcasted_iota(jnp.int32, (V7X_SUBLANES, c), 0)

    def start_block(first):
        @pl.when(first)
        def _():
            xbuf[0:halo, :] = jnp.zeros((halo, c), F32)
            hcar[...] = jnp.zeros_like(hcar)

        @pl.when(jnp.logical_not(first))
        def _():
            xbuf[0:halo, :] = xbuf[ts:ts + halo, :]

    def conv(x_ref):
        xbuf[halo:ts + halo, :] = x_ref[...].astype(F32)
        xc = cb_ref[...] + cw_ref[0:1, :] * xbuf[pl.ds(halo - (CONV_WIDTH - 1), ts), :]
        for k in range(1, CONV_WIDTH):
            xc = xc + cw_ref[k:k + 1, :] * xbuf[pl.ds(halo - (CONV_WIDTH - 1) + k, ts), :]
        return xc

    def gate_blocks(xc, blocks):
        sp = _softplus(-lam_ref[...])
        for n in blocks:
            sl = slice(n * blk, (n + 1) * blk)
            xn = xc[:, sl]
            gates = jnp.dot(xn.astype(BF16), wg_ref[n], preferred_element_type=F32)
            r = jax.nn.sigmoid(gates[:, :blk] + br_ref[:, sl])
            ig = jax.nn.sigmoid(gates[:, blk:] + bi_ref[:, sl])
            log_a = (-LRU_C * r) * sp[:, sl]
            a = jnp.exp(log_a)
            a_scr[:, sl] = a
            u_scr[:, sl] = jnp.sqrt(-jnp.tanh(log_a) * (1.0 + a * a)) * (ig * xn)

    def scan_rows(state, g_ref, slot, row0, nrows):
        for g in range(row0 // V7X_SUBLANES, (row0 + nrows) // V7X_SUBLANES):
            rows = slice(g * V7X_SUBLANES, (g + 1) * V7X_SUBLANES)
            a = a_scr[rows, :]
            u = u_scr[rows, :]
            for sh in (1, 2, 4):
                a_prev = jnp.where(row >= sh, pltpu.roll(a, sh, 0), 1.0)
                u_prev = jnp.where(row >= sh, pltpu.roll(u, sh, 0), 0.0)
                u = a * u_prev + u
                a = a * a_prev
            hh = a * state + u
            u_scr[rows, :] = hh
            state = hh[V7X_SUBLANES - 1:V7X_SUBLANES, :]
        rows = slice(row0, row0 + nrows)
        ylru_scr[slot, rows, :] = (u_scr[rows, :] * jax.nn.gelu(g_ref[rows, :].astype(F32))
                                   ).astype(BF16)
        return state

    @pl.when(i == 0)
    def _():
        xbuf[0:halo, :] = jnp.zeros((halo, c), F32)
        hcar[...] = jnp.zeros_like(hcar)
        gate_blocks(conv(xr0_ref), range(LRU_BLOCKS))
        hcar[...] = scan_rows(hcar[...], gr0_ref, 0, 0, ts)

    nxt = jnp.minimum(i + 1, pl.num_programs(0) - 1)
    start_block(nxt % blocks_per_seq == 0)

    d = o_ref.shape[1]
    n_slices = LRU_BLOCKS
    width = d // n_slices
    gate_refs = ((ga0_ref, gl0_ref), (ga1_ref, gl1_ref))
    per_ref = n_slices // 2
    xc = conv(xr_ref)
    ya = ya_ref[...]
    yl = ylru_scr[i % 2]
    merged = []
    for k in range(n_slices):
        cols = slice(k * width, (k + 1) * width)
        pa = jnp.dot(ya, wa_ref[:, cols], preferred_element_type=F32)
        plru = jnp.dot(yl, wl_ref[:, cols], preferred_element_type=F32)
        gate_blocks(xc, [k])
        ga_ref, gl_ref = gate_refs[k // per_ref]
        gcols = slice((k % per_ref) * width, (k % per_ref + 1) * width)
        merged.append((jax.nn.sigmoid(ga_ref[:, gcols].astype(F32)) * pa
                       + jax.nn.sigmoid(gl_ref[:, gcols].astype(F32)) * plru).astype(BF16))
    merged = jnp.concatenate(merged, axis=1)

    state = hcar[...]
    rows_per = ts // n_slices
    for k in range(n_slices):
        cols = slice(k * width, (k + 1) * width)
        out = jnp.dot(merged, wo_ref[:, cols], preferred_element_type=F32)
        o_ref[:, cols] = h_ref[:, cols] + mod_ref[2:3, cols] * out
        state = scan_rows(state, gr_ref, (i + 1) % 2, k * rows_per, rows_per)
    hcar[...] = state


def _mixer_out(h2d, y_attn, proj, xr_col, mod4, conv_w, conv_b, w_gates, b_r, b_i, lam,
               w_ba, w_bl, w_o, seq):
    m, d = h2d.shape
    tm = ATTN_T
    nq = seq // tm
    wa = y_attn.shape[-1]
    c = conv_w.shape[1]
    half = d // 2
    xr_blk = xr_col // c
    g0 = (xr_col + 2 * c) // half
    n_blocks = m // tm
    resident = pl.Buffered(1)

    def attn_block(i):
        ii = i % nq
        first = ii < nq // 2
        return (i // nq, jnp.where(first, ii, nq - 1 - ii), jnp.where(first, 0, 1), 0, 0)

    def whole(shape):
        return pl.BlockSpec(shape, lambda i: (0,) * len(shape), pipeline_mode=resident)

    return pl.pallas_call(
        functools.partial(_mixer_out_kernel, blocks_per_seq=nq),
        out_shape=jax.ShapeDtypeStruct((m, d), F32),
        grid=(m // tm,),
        in_specs=[
            pl.BlockSpec((tm, d), lambda i: (i, 0)),
            pl.BlockSpec((None, None, None, tm, wa), attn_block),
            pl.BlockSpec((tm, c), lambda i: (0, xr_blk)),
            pl.BlockSpec((tm, c), lambda i: (0, xr_blk + 1)),
            pl.BlockSpec((tm, c), lambda i: (jnp.minimum(i + 1, n_blocks - 1), xr_blk)),
            pl.BlockSpec((tm, c), lambda i: (jnp.minimum(i + 1, n_blocks - 1), xr_blk + 1)),
            pl.BlockSpec((tm, half), lambda i: (i, g0)),
            pl.BlockSpec((tm, half), lambda i: (i, g0 + 1)),
            pl.BlockSpec((tm, half), lambda i: (i, g0 + 2)),
            pl.BlockSpec((tm, half), lambda i: (i, g0 + 3)),
            pl.BlockSpec((None, None, 3, d), lambda i: (i // nq, 1, 0, 0)),
            whole((CONV_WIDTH, c)),
            whole((1, c)),
            whole(w_gates.shape),
            whole((1, c)),
            whole((1, c)),
            whole((1, c)),
            whole((wa, d)),
            whole((c, d)),
            whole((d, d)),
        ],
        out_specs=pl.BlockSpec((tm, d), lambda i: (i, 0)),
        scratch_shapes=[pltpu.VMEM((tm + V7X_SUBLANES, c), F32),
                        pltpu.VMEM((1, c), F32),
                        pltpu.VMEM((tm, c), F32),
                        pltpu.VMEM((tm, c), F32),
                        pltpu.VMEM((2, tm, c), BF16)],
        compiler_params=_params("arbitrary"),
        name="lru_merge_outproj",
    )(h2d, y_attn, proj, proj, proj, proj, proj, proj, proj, proj, mod4, conv_w,
      conv_b.reshape(1, c),
      w_gates, b_r.reshape(1, c), b_i.reshape(1, c), lam.reshape(1, c), w_ba, w_bl, w_o)


def kernel(x, c, w_ada, b_ada, norm_ffn1, w_ffn1_in, w_ffn1_out, norm_mix, w_in,
           conv_w, conv_b, w_rg_gate, b_rg_gate, w_in_gate, b_in_gate, lru_lambda,
           w_branch_attn, w_branch_lru, w_out, norm_ffn2, w_ffn2_in, w_ffn2_out,
           norm_final):
    batch, seq, d = x.shape
    depth = w_ada.shape[0]
    assert depth >= 1, "the final RMSNorm is fused into the last layer"
    attn_width = N_ATTN_HEADS * ATTN_HEAD_DIM
    lru_width = conv_w.shape[-1]
    xr_col = 3 * attn_width
    gate_col = xr_col + 2 * lru_width
    h = x.reshape(batch * seq, d)
    ones = jnp.ones((d,), F32)
    for l in range(depth):
        last = l == depth - 1
        mod = _modulation(c, w_ada[l], b_ada[l])
        mod4 = mod.reshape(batch, N_MOD // 3, 3, d)

        h, (w_in_bf,) = _ffn(h, mod4, 0, norm_ffn1[l], w_ffn1_in[l].astype(BF16),
                             w_ffn1_out[l].astype(BF16), ones, seq, False,
                             to_cast=((w_in, l),))

        proj = _inproj(h, mod4, norm_mix[l], w_in_bf, seq)
        y_attn, (w_ba, w_bl, w_o, w_f2_in, w_f2_out) = _attention(
            proj, batch, seq,
            to_cast=((w_branch_attn, l), (w_branch_lru, l), (w_out, l), (w_ffn2_in, l),
                     (w_ffn2_out, l)))
        w_gates = jnp.concatenate([w_rg_gate[l], w_in_gate[l]], axis=-1).astype(BF16)
        h = _mixer_out(h, y_attn, proj, xr_col, mod4, conv_w[l], conv_b[l], w_gates,
                       b_rg_gate[l], b_in_gate[l], lru_lambda[l], w_ba, w_bl, w_o, seq)

        h, _ = _ffn(h, mod4, 2, norm_ffn2[l], w_f2_in, w_f2_out,
                    norm_final if last else ones, seq, last)
    return h.reshape(batch, seq, d)
```
